```python
import math
import jax, jax.numpy as jnp
from jax import lax
import numpy as np

D_MODEL = 1024
BATCH = 16
SEQ = 2048
DEPTH = 1

CHUNK = 64
DA_HEADS = D_MODEL // 128
DA_HEAD_DIM = 64
DA_V_DIM = 2 * DA_HEAD_DIM
DA_QK_WIDTH = DA_HEADS * 2 * DA_HEAD_DIM
DA_V_WIDTH = DA_HEADS * DA_V_DIM
ROT_DIM = DA_HEAD_DIM // 4
ROPE_THETA = 500000.0
Q_BLOCK = 128
GM_GROUPS = D_MODEL // 128
GM_GROUP_DIM = 128
GM_WIDTH = GM_GROUPS * GM_GROUP_DIM
GM_BLOCK = 128
N_BRANCHES = 2
D_FF = -(-8 * D_MODEL // (3 * 256)) * 256
EPS = 1e-6
SUBLN_EPS = 1e-5
IN_WIDTHS = (DA_QK_WIDTH, DA_QK_WIDTH, DA_V_WIDTH, GM_WIDTH, GM_WIDTH, N_BRANCHES * D_MODEL)
IN_SPLITS = tuple(int(s) for s in np.cumsum(IN_WIDTHS)[:-1])
D_IN = int(sum(IN_WIDTHS))

kernel_name = "hybrid_diffattn_gmlp_gated_block"


def rmsnorm(x, g, eps=EPS):
    xf = x.astype(jnp.float32)
    y = xf * lax.rsqrt(jnp.mean(xf * xf, axis=-1, keepdims=True) + eps)
    return (y * g.astype(jnp.float32)).astype(x.dtype)


def lambda_init_fn(layer):
    return 0.8 - 0.6 * math.exp(-0.3 * layer)


def rope_tables(positions, dtype):
    inv_freq = ROPE_THETA ** (-np.arange(0, ROT_DIM, 2, dtype=np.float32) / ROT_DIM)
    ang = positions.astype(jnp.float32)[..., None] * jnp.asarray(inv_freq)
    return jnp.cos(ang)[:, :, None, :].astype(dtype), jnp.sin(ang)[:, :, None, :].astype(dtype)


def partial_rope(x, cos, sin):
    half = ROT_DIM // 2
    x1, x2, rest = x[..., :half], x[..., half:ROT_DIM], x[..., ROT_DIM:]
    return jnp.concatenate([x1 * cos - x2 * sin, x2 * cos + x1 * sin, rest], axis=-1)


def diff_attention(q1, q2, k1, k2, v, lam):
    S = q1.shape[2]
    scale = DA_HEAD_DIM ** -0.5
    outs = []
    for qb in range(S // Q_BLOCK):
        lo, hi = qb * Q_BLOCK, (qb + 1) * Q_BLOCK
        qpos = np.arange(lo, hi)
        kpos = np.arange(hi)
        mask = (kpos[None, :] // CHUNK) <= (qpos[:, None] // CHUNK)

        def probs(q, k):
            s = jnp.einsum('bhqd,bhkd->bhqk', q[:, :, lo:hi], k[:, :, :hi]).astype(jnp.float32) * scale
            s = jnp.where(mask, s, -jnp.inf)
            return jax.nn.softmax(s, axis=-1)

        p = probs(q1, k1) - lam * probs(q2, k2)
        outs.append(jnp.einsum('bhqk,bhkd->bhqd', p.astype(v.dtype), v[:, :, :hi]))
    return jnp.concatenate(outs, axis=2)


def spatial_gating(u, vg, g_norm, w_s, b_s):
    B, S, _ = vg.shape
    vn = rmsnorm(vg, g_norm)
    vb = vn.reshape(B, S // GM_BLOCK, GM_BLOCK, GM_GROUPS, GM_GROUP_DIM)
    pos = np.arange(GM_BLOCK)
    mask = (pos[None, :] // CHUNK) <= (pos[:, None] // CHUNK)
    w = jnp.where(mask, w_s, 0.0)
    mixed = jnp.einsum('gij,bnjgc->bnigc', w, vb) + b_s.T[None, None, :, :, None]
    return u * mixed.reshape(B, S, GM_WIDTH)


def swiglu(h, w_in, w_out):
    a, b = jnp.split(h @ w_in, 2, axis=-1)
    return (jax.nn.silu(a) * b) @ w_out


def setup_inputs(seed: int = 0) -> dict:
    key = jax.random.key(seed)
    ks = jax.random.split(key, 16)
    f32 = jnp.float32
    x = jax.random.normal(ks[0], (BATCH, SEQ, D_MODEL), f32)
    offsets = jax.random.randint(ks[1], (BATCH,), 0, 4096, dtype=jnp.int32)
    positions = (jnp.arange(SEQ, dtype=jnp.int32)[None, :] + offsets[:, None]).astype(jnp.int32)
    norm_mix_g = 1.0 + 0.02 * jax.random.normal(ks[2], (DEPTH, D_MODEL), f32)
    w_in = jax.random.normal(ks[3], (DEPTH, D_MODEL, D_IN), f32) * D_MODEL ** -0.5
    gate_b = 0.02 * jax.random.normal(ks[4], (DEPTH, N_BRANCHES, D_MODEL), f32)
    lambdas = 0.1 * jax.random.normal(ks[5], (DEPTH, 4, DA_HEAD_DIM), f32)
    subln_g = 1.0 + 0.02 * jax.random.normal(ks[6], (DEPTH, DA_V_DIM), f32)
    gm_norm_g = 1.0 + 0.02 * jax.random.normal(ks[7], (DEPTH, GM_WIDTH), f32)
    gm_ws = jax.random.normal(ks[8], (DEPTH, GM_GROUPS, GM_BLOCK, GM_BLOCK), f32) * GM_BLOCK ** -0.5
    gm_bs = 1.0 + 0.02 * jax.random.normal(ks[9], (DEPTH, GM_GROUPS, GM_BLOCK), f32)
    w_out = jax.random.normal(ks[10], (DEPTH, D_MODEL, D_MODEL), f32) * D_MODEL ** -0.5
    norm_ffn_g = 1.0 + 0.02 * jax.random.normal(ks[11], (DEPTH, D_MODEL), f32)
    w_ffn_in = jax.random.normal(ks[12], (DEPTH, D_MODEL, 2 * D_FF), f32) * D_MODEL ** -0.5
    w_ffn_out = jax.random.normal(ks[13], (DEPTH, D_FF, D_MODEL), f32) * D_FF ** -0.5
    norm_final_g = 1.0 + 0.02 * jax.random.normal(ks[14], (D_MODEL,), f32)
    return {"x": x, "positions": positions, "norm_mix_g": norm_mix_g, "w_in": w_in,
            "gate_b": gate_b, "lambdas": lambdas, "subln_g": subln_g, "gm_norm_g": gm_norm_g,
            "gm_ws": gm_ws, "gm_bs": gm_bs, "w_out": w_out, "norm_ffn_g": norm_ffn_g,
            "w_ffn_in": w_ffn_in, "w_ffn_out": w_ffn_out, "norm_final_g": norm_final_g}


def reference(x, positions, norm_mix_g, w_in, gate_b, lambdas, subln_g, gm_norm_g,
              gm_ws, gm_bs, w_out, norm_ffn_g, w_ffn_in, w_ffn_out, norm_final_g):
    B, S, _ = x.shape
    cos, sin = rope_tables(positions, x.dtype)
    for l in range(DEPTH):
        lam_init = lambda_init_fn(l)
        h = rmsnorm(x, norm_mix_g[l])
        proj = h @ w_in[l]
        q, k, v, gu, gv, gates = jnp.split(proj, IN_SPLITS, axis=-1)

        q = q.reshape(B, S, DA_HEADS, 2, DA_HEAD_DIM)
        k = k.reshape(B, S, DA_HEADS, 2, DA_HEAD_DIM)
        to_bhsd = lambda t: partial_rope(t, cos, sin).transpose(0, 2, 1, 3)
        q1, q2 = to_bhsd(q[..., 0, :]), to_bhsd(q[..., 1, :])
        k1, k2 = to_bhsd(k[..., 0, :]), to_bhsd(k[..., 1, :])
        vh = v.reshape(B, S, DA_HEADS, DA_V_DIM).transpose(0, 2, 1, 3)
        lf = lambdas[l].astype(jnp.float32)
        lam = jnp.exp(jnp.sum(lf[0] * lf[1])) - jnp.exp(jnp.sum(lf[2] * lf[3])) + lam_init
        o = diff_attention(q1, q2, k1, k2, vh, lam).transpose(0, 2, 1, 3)
        o = rmsnorm(o, subln_g[l], SUBLN_EPS) * (1.0 - lam_init)
        attn_out = o.reshape(B, S, DA_V_WIDTH)

        gm_out = spatial_gating(jax.nn.gelu(gu, approximate=False), jax.nn.gelu(gv, approximate=False),
                                gm_norm_g[l], gm_ws[l], gm_bs[l])

        g = jax.nn.sigmoid(gates.reshape(B, S, N_BRANCHES, D_MODEL) + gate_b[l])
        merged = g[:, :, 0, :] * attn_out + g[:, :, 1, :] * gm_out
        x = x + merged @ w_out[l]

        x = x + swiglu(rmsnorm(x, norm_ffn_g[l]), w_ffn_in[l], w_ffn_out[l])
    return rmsnorm(x, norm_final_g)
```

```python
import functools
import math

import jax
import jax.numpy as jnp
import numpy as np
from jax import lax
from jax.experimental import pallas as pl
from jax.experimental.pallas import tpu as pltpu

D_MODEL = 1024
HEADS = 8
HEAD_DIM = 64
V_DIM = 128
ROT_DIM = 16
ROT_HALF = ROT_DIM // 2
ROPE_THETA = 500000.0
CHUNK = 64
GM_GROUPS = 8
GM_BLOCK = 128
D_FF = 2816
EPS = 1e-6
SUBLN_EPS = 1e-5
LAMBDA_INIT = 0.8 - 0.6 * math.exp(-0.3 * 0)
N_SECTIONS = 7

V7X_VMEM_LIMIT_BYTES = 58 * 1024 * 1024

PROJ_ROWS = 512
ATTN_BLOCK = 256
TAIL_ROWS = 256
FFN_CHUNKS = ((0, 1536), (1536, 1280))


def _rmsnorm(x, g, eps):
    return (x * lax.rsqrt(jnp.mean(x * x, axis=-1, keepdims=True) + eps)) * g


def _gelu(x):
    return 0.5 * x * (1.0 + lax.erf(x * np.float32(np.sqrt(0.5))))


def _rope_transposed(xt, cos, sin):
    pieces = []
    for g in range(D_MODEL // HEAD_DIM):
        base = g * HEAD_DIM
        x1 = xt[base:base + ROT_HALF]
        x2 = xt[base + ROT_HALF:base + ROT_DIM]
        pieces.append(x1 * cos - x2 * sin)
        pieces.append(x2 * cos + x1 * sin)
        pieces.append(xt[base + ROT_DIM:base + HEAD_DIM])
    return jnp.concatenate(pieces, axis=0)


def _proj_kernel(x_ref, pos_ref, invf_ref, g_ref, w_ref, gmg_ref,
                 qt_ref, k_ref, vt_ref, u_ref, vn_ref, gates_ref):
    x = x_ref[...]
    h = _rmsnorm(x, g_ref[...], EPS).astype(jnp.bfloat16)

    ang = invf_ref[...] * pos_ref[...].astype(jnp.float32)
    cos = jnp.cos(ang)
    sin = jnp.sin(ang)

    def section(s):
        return jnp.dot(h, w_ref[:, s * D_MODEL:(s + 1) * D_MODEL],
                       preferred_element_type=jnp.float32)

    n_blk = PROJ_ROWS // ATTN_BLOCK

    qt = _rope_transposed(section(0).T, cos, sin) * np.float32(HEAD_DIM ** -0.5)
    qt = qt.astype(jnp.bfloat16)
    for c in range(n_blk):
        qt_ref[0, c] = qt[:, c * ATTN_BLOCK:(c + 1) * ATTN_BLOCK]

    kt = _rope_transposed(section(1).T, cos, sin)
    k_ref[...] = kt.T.astype(jnp.bfloat16)

    vt = section(2).T.astype(jnp.bfloat16)
    for c in range(n_blk):
        vt_ref[0, c] = vt[:, c * ATTN_BLOCK:(c + 1) * ATTN_BLOCK]

    u_ref[...] = _gelu(section(3)).astype(jnp.bfloat16)
    vn_ref[...] = _rmsnorm(_gelu(section(4)), gmg_ref[...], EPS).astype(jnp.bfloat16)
    gates_ref[:, 0:D_MODEL] = section(5).astype(jnp.bfloat16)
    gates_ref[:, D_MODEL:2 * D_MODEL] = section(6).astype(jnp.bfloat16)


def _attn_kernel(qt_ref, k_ref, vt_ref, lam_ref, sg_ref, o_ref, m_ref, l_ref, acc_ref):
    qi = pl.program_id(2)
    tq = ATTN_BLOCK

    qt = qt_ref[0, 0]
    row = lax.broadcasted_iota(jnp.int32, qt.shape, 0)
    zero = jnp.zeros_like(qt)
    q_both = jnp.concatenate([jnp.where(row < HEAD_DIM, qt, zero),
                              jnp.where(row >= HEAD_DIM, qt, zero)], axis=1)

    def scores(j):
        kb = k_ref[0, pl.ds(pl.multiple_of(j * tq, tq), tq), :]
        return jnp.dot(kb, q_both, preferred_element_type=jnp.float32)

    kr = lax.broadcasted_iota(jnp.int32, (tq, tq), 0) // CHUNK
    qc = lax.broadcasted_iota(jnp.int32, (tq, tq), 1) // CHUNK
    allowed = kr <= qc
    s = scores(qi)
    vtb = vt_ref[0, qi]
    for mp in range(2):
        sm = jnp.where(allowed, s[:, mp * tq:(mp + 1) * tq], -jnp.inf)
        m = jnp.max(sm, axis=0, keepdims=True)
        e = jnp.exp(sm - m)
        m_ref[mp] = m
        l_ref[mp] = jnp.sum(e, axis=0, keepdims=True)
        acc_ref[mp] = jnp.dot(vtb, e.astype(jnp.bfloat16), preferred_element_type=jnp.float32)

    def body(j, carry):
        s = scores(j)
        vtb = vt_ref[0, j]
        for mp in range(2):
            sm = s[:, mp * tq:(mp + 1) * tq]
            m_old = m_ref[mp]
            m_new = jnp.maximum(m_old, jnp.max(sm, axis=0, keepdims=True))
            alpha = jnp.exp(m_old - m_new)
            e = jnp.exp(sm - m_new)
            m_ref[mp] = m_new
            l_ref[mp] = alpha * l_ref[mp] + jnp.sum(e, axis=0, keepdims=True)
            acc_ref[mp] = alpha * acc_ref[mp] + jnp.dot(
                vtb, e.astype(jnp.bfloat16), preferred_element_type=jnp.float32)
        return carry

    lax.fori_loop(0, qi, body, 0)

    lf = lam_ref[...]
    lam = (jnp.exp(jnp.sum(lf[0:1] * lf[1:2], axis=-1, keepdims=True))
           - jnp.exp(jnp.sum(lf[2:3] * lf[3:4], axis=-1, keepdims=True))
           + np.float32(LAMBDA_INIT))
    ot = acc_ref[0] / l_ref[0] - lam * (acc_ref[1] / l_ref[1])
    ot = ot * lax.rsqrt(jnp.mean(ot * ot, axis=0, keepdims=True) + SUBLN_EPS)
    o = (ot.T * sg_ref[...]) * np.float32(1.0 - LAMBDA_INIT)
    o_ref[0] = o.astype(o_ref.dtype)


def _tail_kernel(x_ref, attn_ref, u_ref, vn_ref, gates_ref, gb_ref, ws_ref, bs_ref,
                 wo_ref, g2_ref, wfi_ref, wfo_ref, gf_ref, out_ref, gm_ref):
    n_blk = TAIL_ROWS // GM_BLOCK
    pi = lax.broadcasted_iota(jnp.int32, (GM_BLOCK, GM_BLOCK), 0) // CHUNK
    pj = lax.broadcasted_iota(jnp.int32, (GM_BLOCK, GM_BLOCK), 1) // CHUNK
    causal = pj <= pi
    for g in range(GM_GROUPS):
        cols = slice(g * GM_BLOCK, (g + 1) * GM_BLOCK)
        wm = jnp.where(causal, ws_ref[g], 0.0).astype(jnp.bfloat16)
        rhs = jnp.concatenate(
            [vn_ref[nb * GM_BLOCK:(nb + 1) * GM_BLOCK, cols] for nb in range(n_blk)], axis=1)
        mix = jnp.dot(wm, rhs, preferred_element_type=jnp.float32)
        for nb in range(n_blk):
            rows = slice(nb * GM_BLOCK, (nb + 1) * GM_BLOCK)
            mixed = mix[:, nb * GM_BLOCK:(nb + 1) * GM_BLOCK] + bs_ref[:, cols]
            gm_ref[rows, cols] = u_ref[rows, cols].astype(jnp.float32) * mixed

    gates = gates_ref[...].astype(jnp.float32) + gb_ref[...]
    merged = (jax.nn.sigmoid(gates[:, 0:D_MODEL]) * attn_ref[...].astype(jnp.float32)
              + jax.nn.sigmoid(gates[:, D_MODEL:2 * D_MODEL]) * gm_ref[...])
    x1 = x_ref[...] + jnp.dot(merged.astype(jnp.bfloat16), wo_ref[...],
                              preferred_element_type=jnp.float32)

    h2 = _rmsnorm(x1, g2_ref[...], EPS).astype(jnp.bfloat16)
    ff = None
    for (c0, cw) in FFN_CHUNKS:
        a = jnp.dot(h2, wfi_ref[:, c0:c0 + cw], preferred_element_type=jnp.float32)
        b = jnp.dot(h2, wfi_ref[:, D_FF + c0:D_FF + c0 + cw], preferred_element_type=jnp.float32)
        act = (a * jax.nn.sigmoid(a) * b).astype(jnp.bfloat16)
        part = jnp.dot(act, wfo_ref[c0:c0 + cw, :], preferred_element_type=jnp.float32)
        ff = part if ff is None else ff + part
    x2 = x1 + ff
    out_ref[...] = _rmsnorm(x2, gf_ref[...], EPS)


def _resident(shape):
    return pl.BlockSpec(shape, lambda *_: (0,) * len(shape), pipeline_mode=pl.Buffered(1))


def kernel(x, positions, norm_mix_g, w_in, gate_b, lambdas, subln_g, gm_norm_g, gm_ws, gm_bs,
           w_out, norm_ffn_g, w_ffn_in, w_ffn_out, norm_final_g):
    B, S, D = x.shape
    T = B * S
    assert D == D_MODEL and S % PROJ_ROWS == 0 and S % ATTN_BLOCK == 0 and S % TAIL_ROWS == 0
    assert norm_mix_g.shape[0] == 1, "single-layer block"
    nq = S // ATTN_BLOCK
    bf16 = jnp.bfloat16

    x2d = x.reshape(T, D)
    pos_row = positions.reshape(1, T)
    inv_freq = ROPE_THETA ** (-np.arange(0, ROT_DIM, 2, dtype=np.float32) / ROT_DIM)
    invf_col = jnp.asarray(inv_freq, jnp.float32).reshape(ROT_HALF, 1)

    tiles_per_seq = S // PROJ_ROWS
    blk_per_tile = PROJ_ROWS // ATTN_BLOCK
    row_spec = pl.BlockSpec((PROJ_ROWS, D), lambda i: (i, 0))
    t_spec = pl.BlockSpec((1, blk_per_tile, D, ATTN_BLOCK),
                          lambda i: (i // tiles_per_seq, i % tiles_per_seq, 0, 0))
    qt, k, vt, u, vn, gates = pl.pallas_call(
        _proj_kernel,
        grid=(T // PROJ_ROWS,),
        in_specs=[row_spec,
                  pl.BlockSpec((1, PROJ_ROWS), lambda i: (0, i)),
                  _resident((ROT_HALF, 1)),
                  _resident((1, D)),
                  _resident((D, N_SECTIONS * D)),
                  _resident((1, D))],
        out_specs=[t_spec, row_spec, t_spec, row_spec, row_spec,
                   pl.BlockSpec((PROJ_ROWS, 2 * D), lambda i: (i, 0))],
        out_shape=[jax.ShapeDtypeStruct((B, nq, D, ATTN_BLOCK), bf16),
                   jax.ShapeDtypeStruct((T, D), bf16),
                   jax.ShapeDtypeStruct((B, nq, D, ATTN_BLOCK), bf16),
                   jax.ShapeDtypeStruct((T, D), bf16),
                   jax.ShapeDtypeStruct((T, D), bf16),
                   jax.ShapeDtypeStruct((T, 2 * D), bf16)],
        compiler_params=pltpu.CompilerParams(
            dimension_semantics=("arbitrary",), vmem_limit_bytes=V7X_VMEM_LIMIT_BYTES),
        name="proj",
    )(x2d, pos_row, invf_col, norm_mix_g, w_in[0].astype(bf16), gm_norm_g)

    attn = pl.pallas_call(
        _attn_kernel,
        grid=(B, HEADS, nq),
        in_specs=[pl.BlockSpec((1, 1, V_DIM, ATTN_BLOCK), lambda b, h, q: (b, q, h, 0)),
                  pl.BlockSpec((1, S, V_DIM), lambda b, h, q: (b, 0, h)),
                  pl.BlockSpec((1, nq, V_DIM, ATTN_BLOCK), lambda b, h, q: (b, 0, h, 0)),
                  _resident((4, HEAD_DIM)),
                  _resident((1, V_DIM))],
        out_specs=pl.BlockSpec((1, ATTN_BLOCK, V_DIM), lambda b, h, q: (b, q, h)),
        out_shape=jax.ShapeDtypeStruct((B, S, D), bf16),
        scratch_shapes=[pltpu.VMEM((2, 1, ATTN_BLOCK), jnp.float32),
                        pltpu.VMEM((2, 1, ATTN_BLOCK), jnp.float32),
                        pltpu.VMEM((2, V_DIM, ATTN_BLOCK), jnp.float32)],
        compiler_params=pltpu.CompilerParams(
            dimension_semantics=("arbitrary", "arbitrary", "arbitrary"),
            vmem_limit_bytes=V7X_VMEM_LIMIT_BYTES),
        name="diff_attn",
    )(qt, k.reshape(B, S, D), vt, lambdas[0], subln_g)

    bs_cols = jnp.repeat(gm_bs[0].T, GM_BLOCK, axis=1)
    row_spec = pl.BlockSpec((TAIL_ROWS, D), lambda i: (i, 0))
    out = pl.pallas_call(
        _tail_kernel,
        grid=(T // TAIL_ROWS,),
        in_specs=[row_spec, row_spec, row_spec, row_spec,
                  pl.BlockSpec((TAIL_ROWS, 2 * D), lambda i: (i, 0)),
                  _resident((1, 2 * D)),
                  _resident((GM_GROUPS, GM_BLOCK, GM_BLOCK)),
                  _resident((GM_BLOCK, D)),
                  _resident((D, D)),
                  _resident((1, D)),
                  _resident((D, 2 * D_FF)),
                  _resident((D_FF, D)),
                  _resident((1, D))],
        out_specs=row_spec,
        out_shape=jax.ShapeDtypeStruct((T, D), jnp.float32),
        scratch_shapes=[pltpu.VMEM((TAIL_ROWS, D), jnp.float32)],
        compiler_params=pltpu.CompilerParams(
            dimension_semantics=("arbitrary",), vmem_limit_bytes=V7X_VMEM_LIMIT_BYTES),
        name="tail",
    )(x2d, attn.reshape(T, D), u, vn, gates, gate_b[0].reshape(1, 2 * D), gm_ws[0], bs_cols,
      w_out[0].astype(bf16), norm_ffn_g, w_ffn_in[0].astype(bf16), w_ffn_out[0].astype(bf16),
      norm_final_g.reshape(1, D))
    return out.reshape(B, S, D)
```

```python
import functools
import math

import jax
import jax.numpy as jnp
import numpy as np
from jax import lax
from jax.experimental import pallas as pl
from jax.experimental.pallas import tpu as pltpu

D_MODEL = 1024
HEADS = 8
HEAD_DIM = 64
V_DIM = 128
ROT_DIM = 16
ROT_HALF = ROT_DIM // 2
ROPE_THETA = 500000.0
CHUNK = 64
GM_GROUPS = 8
GM_BLOCK = 128
D_FF = 2816
EPS = 1e-6
SUBLN_EPS = 1e-5
LAMBDA_INIT = 0.8 - 0.6 * math.exp(-0.3 * 0)
N_SECTIONS = 7

V7X_VMEM_LIMIT_BYTES = 58 * 1024 * 1024

PROJ_ROWS = 512
ATTN_BLOCK = 256
ACC_ROWS = V_DIM + 16
TAIL_ROWS = 256
FFN_CHUNKS = ((0, 1536), (1536, 1280))


def _rmsnorm(x, g, eps):
    return (x * lax.rsqrt(jnp.mean(x * x, axis=-1, keepdims=True) + eps)) * g


def _gelu(x):
    return 0.5 * x * (1.0 + lax.erf(x * np.float32(np.sqrt(0.5))))


def _rope_transposed(xt, cos, sin):
    pieces = []
    for g in range(D_MODEL // HEAD_DIM):
        base = g * HEAD_DIM
        x1 = xt[base:base + ROT_HALF]
        x2 = xt[base + ROT_HALF:base + ROT_DIM]
        pieces.append(x1 * cos - x2 * sin)
        pieces.append(x2 * cos + x1 * sin)
        pieces.append(xt[base + ROT_DIM:base + HEAD_DIM])
    return jnp.concatenate(pieces, axis=0)


def _proj_kernel(x_ref, pos_ref, invf_ref, g_ref, w_ref, gmg_ref,
                 qt_ref, k_ref, vt_ref, u_ref, vn_ref, gates_ref):
    x = x_ref[...]
    h = _rmsnorm(x, g_ref[...], EPS).astype(jnp.bfloat16)

    ang = invf_ref[...] * pos_ref[...].astype(jnp.float32)
    cos = jnp.cos(ang)
    sin = jnp.sin(ang)

    def section(s):
        return jnp.dot(h, w_ref[:, s * D_MODEL:(s + 1) * D_MODEL],
                       preferred_element_type=jnp.float32)

    n_blk = PROJ_ROWS // ATTN_BLOCK

    qt = _rope_transposed(section(0).T, cos, sin) * np.float32(HEAD_DIM ** -0.5 * math.log2(math.e))
    qt = qt.astype(jnp.bfloat16)
    for c in range(n_blk):
        qt_ref[0, c] = qt[:, c * ATTN_BLOCK:(c + 1) * ATTN_BLOCK]

    kt = _rope_transposed(section(1).T, cos, sin)
    k_ref[...] = kt.T.astype(jnp.bfloat16)

    vt = section(2).T.astype(jnp.bfloat16)
    for c in range(n_blk):
        vt_ref[0, c] = vt[:, c * ATTN_BLOCK:(c + 1) * ATTN_BLOCK]

    u_ref[...] = _gelu(section(3)).astype(jnp.bfloat16)
    vn_ref[...] = _rmsnorm(_gelu(section(4)), gmg_ref[...], EPS).astype(jnp.bfloat16)
    gates_ref[:, 0:D_MODEL] = section(5).astype(jnp.bfloat16)
    gates_ref[:, D_MODEL:2 * D_MODEL] = section(6).astype(jnp.bfloat16)


def _attn_kernel(qt_ref, k_ref, vt_ref, lam_ref, sg_ref, o_ref, qb_ref, m_ref, acc_ref):
    qi = pl.program_id(1)
    tq = ATTN_BLOCK

    row = lax.broadcasted_iota(jnp.int32, (V_DIM, tq), 0)
    for h in range(HEADS):
        qt = qt_ref[0, 0, h * V_DIM:(h + 1) * V_DIM, :]
        zero = jnp.zeros_like(qt)
        qb_ref[h, :, 0:tq] = jnp.where(row < HEAD_DIM, qt, zero)
        qb_ref[h, :, tq:2 * tq] = jnp.where(row >= HEAD_DIM, qt, zero)

    kr = lax.broadcasted_iota(jnp.int32, (tq, tq), 0) // CHUNK
    qc = lax.broadcasted_iota(jnp.int32, (tq, tq), 1) // CHUNK
    allowed = kr <= qc
    ones_rows = jnp.ones((ACC_ROWS - V_DIM, tq), jnp.bfloat16)

    def block(j, first):
        rows = pl.ds(pl.multiple_of(j * tq, tq), tq)

        def scores(h):
            return jnp.dot(k_ref[0, rows, h * V_DIM:(h + 1) * V_DIM], qb_ref[h],
                           preferred_element_type=jnp.float32)

        s_next = scores(0)
        for h in range(HEADS):
            cols = slice(h * V_DIM, (h + 1) * V_DIM)
            s = s_next
            if h + 1 < HEADS:
                s_next = scores(h + 1)
            vta = jnp.concatenate([vt_ref[0, j, cols, :], ones_rows], axis=0)
            for mp in range(2):
                sm = s[:, mp * tq:(mp + 1) * tq]
                if first:
                    sm = jnp.where(allowed, sm, -jnp.inf)
                    m_new = jnp.max(sm, axis=0, keepdims=True)
                else:
                    m_old = m_ref[h, mp]
                    m_new = jnp.maximum(m_old, jnp.max(sm, axis=0, keepdims=True))
                    alpha = jnp.exp2(m_old - m_new)
                p = jnp.exp2(sm - m_new).astype(jnp.bfloat16)
                pv = jnp.dot(vta, p, preferred_element_type=jnp.float32)
                m_ref[h, mp] = m_new
                if first:
                    acc_ref[h, mp] = pv
                else:
                    acc_ref[h, mp] = alpha * acc_ref[h, mp] + pv

    block(qi, True)

    def body(j, carry):
        block(j, False)
        return carry

    lax.fori_loop(0, qi, body, 0)

    lf = lam_ref[...]
    lam = (jnp.exp(jnp.sum(lf[0:1] * lf[1:2], axis=-1, keepdims=True))
           - jnp.exp(jnp.sum(lf[2:3] * lf[3:4], axis=-1, keepdims=True))
           + np.float32(LAMBDA_INIT))
    for h in range(HEADS):
        ot = (acc_ref[h, 0, 0:V_DIM] / acc_ref[h, 0, V_DIM:V_DIM + 1]
              - lam * (acc_ref[h, 1, 0:V_DIM] / acc_ref[h, 1, V_DIM:V_DIM + 1]))
        ot = ot * lax.rsqrt(jnp.mean(ot * ot, axis=0, keepdims=True) + SUBLN_EPS)
        o = (ot.T * sg_ref[...]) * np.float32(1.0 - LAMBDA_INIT)
        o_ref[0, :, h * V_DIM:(h + 1) * V_DIM] = o.astype(o_ref.dtype)


def _tail_kernel(x_ref, attn_ref, u_ref, vn_ref, gates_ref, gb_ref, ws_ref, bs_ref,
                 wo_ref, g2_ref, wfi_ref, wfo_ref, gf_ref, out_ref, gm_ref):
    n_blk = TAIL_ROWS // GM_BLOCK
    pi = lax.broadcasted_iota(jnp.int32, (GM_BLOCK, GM_BLOCK), 0) // CHUNK
    pj = lax.broadcasted_iota(jnp.int32, (GM_BLOCK, GM_BLOCK), 1) // CHUNK
    causal = pj <= pi
    for g in range(GM_GROUPS):
        cols = slice(g * GM_BLOCK, (g + 1) * GM_BLOCK)
        wm = jnp.where(causal, ws_ref[g], 0.0).astype(jnp.bfloat16)
        rhs = jnp.concatenate(
            [vn_ref[nb * GM_BLOCK:(nb + 1) * GM_BLOCK, cols] for nb in range(n_blk)], axis=1)
        mix = jnp.dot(wm, rhs, preferred_element_type=jnp.float32)
        for nb in range(n_blk):
            rows = slice(nb * GM_BLOCK, (nb + 1) * GM_BLOCK)
            mixed = mix[:, nb * GM_BLOCK:(nb + 1) * GM_BLOCK] + bs_ref[:, cols]
            gm_ref[rows, cols] = u_ref[rows, cols].astype(jnp.float32) * mixed

    gates = gates_ref[...].astype(jnp.float32) + gb_ref[...]
    merged = (jax.nn.sigmoid(gates[:, 0:D_MODEL]) * attn_ref[...].astype(jnp.float32)
              + jax.nn.sigmoid(gates[:, D_MODEL:2 * D_MODEL]) * gm_ref[...])
    x1 = x_ref[...] + jnp.dot(merged.astype(jnp.bfloat16), wo_ref[...],
                              preferred_element_type=jnp.float32)

    h2 = _rmsnorm(x1, g2_ref[...], EPS).astype(jnp.bfloat16)
    ff = None
    for (c0, cw) in FFN_CHUNKS:
        a = jnp.dot(h2, wfi_ref[:, c0:c0 + cw], preferred_element_type=jnp.float32)
        b = jnp.dot(h2, wfi_ref[:, D_FF + c0:D_FF + c0 + cw], preferred_element_type=jnp.float32)
        act = (a * jax.nn.sigmoid(a) * b).astype(jnp.bfloat16)
        part = jnp.dot(act, wfo_ref[c0:c0 + cw, :], preferred_element_type=jnp.float32)
        ff = part if ff is None else ff + part
    x2 = x1 + ff
    out_ref[...] = _rmsnorm(x2, gf_ref[...], EPS)


def _resident(shape):
    return pl.BlockSpec(shape, lambda *_: (0,) * len(shape), pipeline_mode=pl.Buffered(1))


def kernel(x, positions, norm_mix_g, w_in, gate_b, lambdas, subln_g, gm_norm_g, gm_ws, gm_bs,
           w_out, norm_ffn_g, w_ffn_in, w_ffn_out, norm_final_g):
    B, S, D = x.shape
    T = B * S
    assert D == D_MODEL and S % PROJ_ROWS == 0 and S % ATTN_BLOCK == 0 and S % TAIL_ROWS == 0
    assert norm_mix_g.shape[0] == 1, "single-layer block"
    nq = S // ATTN_BLOCK
    bf16 = jnp.bfloat16

    x2d = x.reshape(T, D)
    pos_row = positions.reshape(1, T)
    inv_freq = ROPE_THETA ** (-np.arange(0, ROT_DIM, 2, dtype=np.float32) / ROT_DIM)
    invf_col = jnp.asarray(inv_freq, jnp.float32).reshape(ROT_HALF, 1)

    tiles_per_seq = S // PROJ_ROWS
    blk_per_tile = PROJ_ROWS // ATTN_BLOCK
    row_spec = pl.BlockSpec((PROJ_ROWS, D), lambda i: (i, 0))
    t_spec = pl.BlockSpec((1, blk_per_tile, D, ATTN_BLOCK),
                          lambda i: (i // tiles_per_seq, i % tiles_per_seq, 0, 0))
    qt, k, vt, u, vn, gates = pl.pallas_call(
        _proj_kernel,
        grid=(T // PROJ_ROWS,),
        in_specs=[row_spec,
                  pl.BlockSpec((1, PROJ_ROWS), lambda i: (0, i)),
                  _resident((ROT_HALF, 1)),
                  _resident((1, D)),
                  _resident((D, N_SECTIONS * D)),
                  _resident((1, D))],
        out_specs=[t_spec, row_spec, t_spec, row_spec, row_spec,
                   pl.BlockSpec((PROJ_ROWS, 2 * D), lambda i: (i, 0))],
        out_shape=[jax.ShapeDtypeStruct((B, nq, D, ATTN_BLOCK), bf16),
                   jax.ShapeDtypeStruct((T, D), bf16),
                   jax.ShapeDtypeStruct((B, nq, D, ATTN_BLOCK), bf16),
                   jax.ShapeDtypeStruct((T, D), bf16),
                   jax.ShapeDtypeStruct((T, D), bf16),
                   jax.ShapeDtypeStruct((T, 2 * D), bf16)],
        compiler_params=pltpu.CompilerParams(
            dimension_semantics=("arbitrary",), vmem_limit_bytes=V7X_VMEM_LIMIT_BYTES),
        name="proj",
    )(x2d, pos_row, invf_col, norm_mix_g, w_in[0].astype(bf16), gm_norm_g)

    attn = pl.pallas_call(
        _attn_kernel,
        grid=(B, nq),
        in_specs=[pl.BlockSpec((1, 1, D, ATTN_BLOCK), lambda b, q: (b, q, 0, 0)),
                  pl.BlockSpec((1, S, D), lambda b, q: (b, 0, 0)),
                  pl.BlockSpec((1, nq, D, ATTN_BLOCK), lambda b, q: (b, 0, 0, 0)),
                  _resident((4, HEAD_DIM)),
                  _resident((1, V_DIM))],
        out_specs=pl.BlockSpec((1, ATTN_BLOCK, D), lambda b, q: (b, q, 0)),
        out_shape=jax.ShapeDtypeStruct((B, S, D), bf16),
        scratch_shapes=[pltpu.VMEM((HEADS, V_DIM, 2 * ATTN_BLOCK), bf16),
                        pltpu.VMEM((HEADS, 2, 1, ATTN_BLOCK), jnp.float32),
                        pltpu.VMEM((HEADS, 2, ACC_ROWS, ATTN_BLOCK), jnp.float32)],
        compiler_params=pltpu.CompilerParams(
            dimension_semantics=("arbitrary", "arbitrary"),
            vmem_limit_bytes=V7X_VMEM_LIMIT_BYTES),
        name="diff_attn",
    )(qt, k.reshape(B, S, D), vt, lambdas[0], subln_g)

    bs_cols = jnp.repeat(gm_bs[0].T, GM_BLOCK, axis=1)
    row_spec = pl.BlockSpec((TAIL_ROWS, D), lambda i: (i, 0))
    out = pl.pallas_call(
        _tail_kernel,
        grid=(T // TAIL_ROWS,),
        in_specs=[row_spec, row_spec, row_spec, row_spec,
                  pl.BlockSpec((TAIL_ROWS, 2 * D), lambda i: (i, 0)),
                  _resident((1, 2 * D)),
                  _resident((GM_GROUPS, GM_BLOCK, GM_BLOCK)),
                  _resident((GM_BLOCK, D)),
                  _resident((D, D)),
                  _resident((1, D)),
                  _resident((D, 2 * D_FF)),
                  _resident((D_FF, D)),
                  _resident((1, D))],
        out_specs=row_spec,
        out_shape=jax.ShapeDtypeStruct((T, D), jnp.float32),
        scratch_shapes=[pltpu.VMEM((TAIL_ROWS, D), jnp.float32)],
        compiler_params=pltpu.CompilerParams(
            dimension_semantics=("arbitrary",), vmem_limit_bytes=V7X_VMEM_LIMIT_BYTES),
        name="tail",
    )(x2d, attn.reshape(T, D), u, vn, gates, gate_b[0].reshape(1, 2 * D), gm_ws[0], bs_cols,
      w_out[0].astype(bf16), norm_ffn_g, w_ffn_in[0].astype(bf16), w_ffn_out[0].astype(bf16),
      norm_final_g.reshape(1, D))
    return out.reshape(B, S, D)
```

```python
import math

import jax
import jax.numpy as jnp
import numpy as np
from jax import lax
from jax.experimental import pallas as pl
from jax.experimental.pallas import tpu as pltpu

D_MODEL = 1024
HEADS = 8
HEAD_DIM = 64
V_DIM = 128
ROT_DIM = 16
ROT_HALF = ROT_DIM // 2
ROPE_THETA = 500000.0
CHUNK = 64
GM_GROUPS = 8
GM_BLOCK = 128
D_FF = 2816
EPS = 1e-6
SUBLN_EPS = 1e-5
LAMBDA_INIT = 0.8 - 0.6 * math.exp(-0.3 * 0)
N_SECTIONS = 7

V7X_VMEM_LIMIT_BYTES = 58 * 1024 * 1024

PROJ_ROWS = 512
ATTN_BLOCK = 256
SCORE_LOOKAHEAD = 6
ACC_ROWS = V_DIM + 16
TAIL_ROWS = 256
FFN_CHUNKS = ((0, 1536), (1536, 1280))


def _rmsnorm(x, g, eps):
    return (x * lax.rsqrt(jnp.mean(x * x, axis=-1, keepdims=True) + eps)) * g


def _gelu(x):
    return 0.5 * x * (1.0 + lax.erf(x * np.float32(np.sqrt(0.5))))


def _rope_transposed(xt, cos, sin):
    pieces = []
    for g in range(D_MODEL // HEAD_DIM):
        base = g * HEAD_DIM
        x1 = xt[base:base + ROT_HALF]
        x2 = xt[base + ROT_HALF:base + ROT_DIM]
        pieces.append(x1 * cos - x2 * sin)
        pieces.append(x2 * cos + x1 * sin)
        pieces.append(xt[base + ROT_DIM:base + HEAD_DIM])
    return jnp.concatenate(pieces, axis=0)


def _proj_kernel(x_ref, pos_ref, invf_ref, g_ref, w_ref, gmg_ref,
                 qt_ref, k_ref, vt_ref, u_ref, vn_ref, gates_ref):
    x = x_ref[...]
    h = _rmsnorm(x, g_ref[...], EPS).astype(jnp.bfloat16)

    ang = invf_ref[...] * pos_ref[...].astype(jnp.float32)
    cos = jnp.cos(ang)
    sin = jnp.sin(ang)

    def section(s):
        return jnp.dot(h, w_ref[:, s * D_MODEL:(s + 1) * D_MODEL],
                       preferred_element_type=jnp.float32)

    n_blk = PROJ_ROWS // ATTN_BLOCK

    qt = _rope_transposed(section(0).T, cos, sin) * np.float32(HEAD_DIM ** -0.5 * math.log2(math.e))
    qt = qt.astype(jnp.bfloat16)
    for c in range(n_blk):
        qt_ref[0, c] = qt[:, c * ATTN_BLOCK:(c + 1) * ATTN_BLOCK]

    kt = _rope_transposed(section(1).T, cos, sin)
    k_ref[...] = kt.T.astype(jnp.bfloat16)

    vt = section(2).T.astype(jnp.bfloat16)
    for c in range(n_blk):
        vt_ref[0, c] = vt[:, c * ATTN_BLOCK:(c + 1) * ATTN_BLOCK]

    u_ref[...] = _gelu(section(3)).astype(jnp.bfloat16)
    vn_ref[...] = _rmsnorm(_gelu(section(4)), gmg_ref[...], EPS).astype(jnp.bfloat16)
    gates_ref[:, 0:D_MODEL] = section(5).astype(jnp.bfloat16)
    gates_ref[:, D_MODEL:2 * D_MODEL] = section(6).astype(jnp.bfloat16)


def _attn_kernel(qt_ref, k_ref, vt_ref, lam_ref, sg_ref, o_ref, m_ref, acc_ref):
    p_id = pl.program_id(1)
    tq = ATTN_BLOCK
    nq = vt_ref.shape[1]
    tile_a = p_id
    tile_b = nq - 1 - p_id
    tiles = (tile_a, tile_b)

    m_ref[0] = jnp.full(m_ref.shape[1:], -jnp.inf, jnp.float32)
    acc_ref[0] = jnp.zeros(acc_ref.shape[1:], jnp.float32)

    kr = lax.broadcasted_iota(jnp.int32, (tq, tq), 0) // CHUNK
    qc = lax.broadcasted_iota(jnp.int32, (tq, tq), 1) // CHUNK
    allowed = kr <= qc
    ones_rows = jnp.ones((ACC_ROWS - V_DIM, tq), jnp.bfloat16)

    slots = [(1, tile_b, tile_b, True, True)]
    for t in range(1, nq):
        is_b = t <= tile_b
        slots.append((is_b.astype(jnp.int32), jnp.where(is_b, tile_b, tile_a),
                      jnp.where(is_b, tile_b - t, t - (tile_b + 1)), False, False))
    slots.append((0, tile_a, tile_a, True, False))

    items = [(t, h, mp) for t in range(len(slots)) for h in range(HEADS) for mp in range(2)]

    def scores(t, h, mp):
        _, tile, blk, _, _ = slots[t]
        rows = pl.ds(pl.multiple_of(blk * tq, tq), tq)
        kb = k_ref[0, rows, h * V_DIM:(h + 1) * V_DIM]
        qh = qt_ref[0, tile, h * V_DIM + mp * HEAD_DIM:h * V_DIM + (mp + 1) * HEAD_DIM, :]
        z = jnp.zeros_like(qh)
        rhs = jnp.concatenate([qh, z] if mp == 0 else [z, qh], axis=0)
        return jnp.dot(kb, rhs, preferred_element_type=jnp.float32)

    pending = [scores(*items[i]) for i in range(SCORE_LOOKAHEAD)]
    for idx, (t, h, mp) in enumerate(items):
        sel, _, blk, masked, first = slots[t]
        sm = pending.pop(0)
        if idx + SCORE_LOOKAHEAD < len(items):
            pending.append(scores(*items[idx + SCORE_LOOKAHEAD]))
        vta = jnp.concatenate([vt_ref[0, blk, h * V_DIM:(h + 1) * V_DIM, :], ones_rows], axis=0)
        if masked:
            sm = jnp.where(allowed, sm, -jnp.inf)
        if first:
            m_new = jnp.max(sm, axis=0, keepdims=True)
        else:
            m_old = m_ref[sel, h, mp]
            m_new = jnp.maximum(m_old, jnp.max(sm, axis=0, keepdims=True))
            alpha = jnp.exp2(m_old - m_new)
        p = jnp.exp2(sm - m_new).astype(jnp.bfloat16)
        pv = jnp.dot(vta, p, preferred_element_type=jnp.float32)
        m_ref[sel, h, mp] = m_new
        if first:
            acc_ref[sel, h, mp] = pv
        else:
            acc_ref[sel, h, mp] = alpha * acc_ref[sel, h, mp] + pv

    lf = lam_ref[...]
    lam = (jnp.exp(jnp.sum(lf[0:1] * lf[1:2], axis=-1, keepdims=True))
           - jnp.exp(jnp.sum(lf[2:3] * lf[3:4], axis=-1, keepdims=True))
           + np.float32(LAMBDA_INIT))
    for sel in range(2):
        out_rows = pl.ds(pl.multiple_of(tiles[sel] * tq, tq), tq)
        for h in range(HEADS):
            ot = (acc_ref[sel, h, 0, 0:V_DIM] / acc_ref[sel, h, 0, V_DIM:V_DIM + 1]
                  - lam * (acc_ref[sel, h, 1, 0:V_DIM] / acc_ref[sel, h, 1, V_DIM:V_DIM + 1]))
            ot = ot * lax.rsqrt(jnp.mean(ot * ot, axis=0, keepdims=True) + SUBLN_EPS)
            o = (ot.T * sg_ref[...]) * np.float32(1.0 - LAMBDA_INIT)
            o_ref[0, out_rows, h * V_DIM:(h + 1) * V_DIM] = o.astype(o_ref.dtype)


def _tail_kernel(x_ref, attn_ref, u_ref, vn_ref, gates_ref, gb_ref, ws_ref, bs_ref,
                 wo_ref, g2_ref, wfi_ref, wfo_ref, gf_ref, out_ref, gm_ref):
    n_blk = TAIL_ROWS // GM_BLOCK
    pi = lax.broadcasted_iota(jnp.int32, (GM_BLOCK, GM_BLOCK), 0) // CHUNK
    pj = lax.broadcasted_iota(jnp.int32, (GM_BLOCK, GM_BLOCK), 1) // CHUNK
    causal = pj <= pi
    for g in range(GM_GROUPS):
        cols = slice(g * GM_BLOCK, (g + 1) * GM_BLOCK)
        wm = jnp.where(causal, ws_ref[g], 0.0).astype(jnp.bfloat16)
        rhs = jnp.concatenate(
            [vn_ref[nb * GM_BLOCK:(nb + 1) * GM_BLOCK, cols] for nb in range(n_blk)], axis=1)
        mix = jnp.dot(wm, rhs, preferred_element_type=jnp.float32)
        for nb in range(n_blk):
            rows = slice(nb * GM_BLOCK, (nb + 1) * GM_BLOCK)
            mixed = mix[:, nb * GM_BLOCK:(nb + 1) * GM_BLOCK] + bs_ref[:, cols]
            gm_ref[rows, cols] = u_ref[rows, cols].astype(jnp.float32) * mixed

    gates = gates_ref[...].astype(jnp.float32) + gb_ref[...]
    merged = (jax.nn.sigmoid(gates[:, 0:D_MODEL]) * attn_ref[...].astype(jnp.float32)
              + jax.nn.sigmoid(gates[:, D_MODEL:2 * D_MODEL]) * gm_ref[...])
    x1 = x_ref[...] + jnp.dot(merged.astype(jnp.bfloat16), wo_ref[...],
                              preferred_element_type=jnp.float32)

    h2 = _rmsnorm(x1, g2_ref[...], EPS).astype(jnp.bfloat16)
    ff = None
    for (c0, cw) in FFN_CHUNKS:
        a = jnp.dot(h2, wfi_ref[:, c0:c0 + cw], preferred_element_type=jnp.float32)
        b = jnp.dot(h2, wfi_ref[:, D_FF + c0:D_FF + c0 + cw], preferred_element_type=jnp.float32)
        act = (a * jax.nn.sigmoid(a) * b).astype(jnp.bfloat16)
        part = jnp.dot(act, wfo_ref[c0:c0 + cw, :], preferred_element_type=jnp.float32)
        ff = part if ff is None else ff + part
    x2 = x1 + ff
    out_ref[...] = _rmsnorm(x2, gf_ref[...], EPS)


def _resident(shape):
    return pl.BlockSpec(shape, lambda *_: (0,) * len(shape), pipeline_mode=pl.Buffered(1))


def kernel(x, positions, norm_mix_g, w_in, gate_b, lambdas, subln_g, gm_norm_g, gm_ws, gm_bs,
           w_out, norm_ffn_g, w_ffn_in, w_ffn_out, norm_final_g):
    B, S, D = x.shape
    T = B * S
    assert D == D_MODEL and S % PROJ_ROWS == 0 and S % ATTN_BLOCK == 0 and S % TAIL_ROWS == 0
    assert norm_mix_g.shape[0] == 1, "single-layer block"
    nq = S // ATTN_BLOCK
    bf16 = jnp.bfloat16

    x2d = x.reshape(T, D)
    pos_row = positions.reshape(1, T)
    inv_freq = ROPE_THETA ** (-np.arange(0, ROT_DIM, 2, dtype=np.float32) / ROT_DIM)
    invf_col = jnp.asarray(inv_freq, jnp.float32).reshape(ROT_HALF, 1)

    tiles_per_seq = S // PROJ_ROWS
    blk_per_tile = PROJ_ROWS // ATTN_BLOCK
    row_spec = pl.BlockSpec((PROJ_ROWS, D), lambda i: (i, 0))
    t_spec = pl.BlockSpec((1, blk_per_tile, D, ATTN_BLOCK),
                          lambda i: (i // tiles_per_seq, i % tiles_per_seq, 0, 0))
    qt, k, vt, u, vn, gates = pl.pallas_call(
        _proj_kernel,
        grid=(T // PROJ_ROWS,),
        in_specs=[row_spec,
                  pl.BlockSpec((1, PROJ_ROWS), lambda i: (0, i)),
                  _resident((ROT_HALF, 1)),
                  _resident((1, D)),
                  _resident((D, N_SECTIONS * D)),
                  _resident((1, D))],
        out_specs=[t_spec, row_spec, t_spec, row_spec, row_spec,
                   pl.BlockSpec((PROJ_ROWS, 2 * D), lambda i: (i, 0))],
        out_shape=[jax.ShapeDtypeStruct((B, nq, D, ATTN_BLOCK), bf16),
                   jax.ShapeDtypeStruct((T, D), bf16),
                   jax.ShapeDtypeStruct((B, nq, D, ATTN_BLOCK), bf16),
                   jax.ShapeDtypeStruct((T, D), bf16),
                   jax.ShapeDtypeStruct((T, D), bf16),
                   jax.ShapeDtypeStruct((T, 2 * D), bf16)],
        compiler_params=pltpu.CompilerParams(
            dimension_semantics=("arbitrary",), vmem_limit_bytes=V7X_VMEM_LIMIT_BYTES),
        name="proj",
    )(x2d, pos_row, invf_col, norm_mix_g, w_in[0].astype(bf16), gm_norm_g)

    assert nq % 2 == 0
    attn = pl.pallas_call(
        _attn_kernel,
        grid=(B, nq // 2),
        in_specs=[pl.BlockSpec((1, nq, D, ATTN_BLOCK), lambda b, p: (b, 0, 0, 0)),
                  pl.BlockSpec((1, S, D), lambda b, p: (b, 0, 0)),
                  pl.BlockSpec((1, nq, D, ATTN_BLOCK), lambda b, p: (b, 0, 0, 0)),
                  _resident((4, HEAD_DIM)),
                  _resident((1, V_DIM))],
        out_specs=pl.BlockSpec((1, S, D), lambda b, p: (b, 0, 0)),
        out_shape=jax.ShapeDtypeStruct((B, S, D), bf16),
        scratch_shapes=[pltpu.VMEM((2, HEADS, 2, 1, ATTN_BLOCK), jnp.float32),
                        pltpu.VMEM((2, HEADS, 2, ACC_ROWS, ATTN_BLOCK), jnp.float32)],
        compiler_params=pltpu.CompilerParams(
            dimension_semantics=("arbitrary", "arbitrary"),
            vmem_limit_bytes=V7X_VMEM_LIMIT_BYTES),
        name="diff_attn",
    )(qt, k.reshape(B, S, D), vt, lambdas[0], subln_g)

    bs_cols = jnp.repeat(gm_bs[0].T, GM_BLOCK, axis=1)
    row_spec = pl.BlockSpec((TAIL_ROWS, D), lambda i: (i, 0))
    out = pl.pallas_call(
        _tail_kernel,
        grid=(T // TAIL_ROWS,),
        in_specs=[row_spec, row_spec, row_spec, row_spec,
                  pl.BlockSpec((TAIL_ROWS, 2 * D), lambda i: (i, 0)),
                  _resident((1, 2 * D)),
                  _resident((GM_GROUPS, GM_BLOCK, GM_BLOCK)),
                  _resident((GM_BLOCK, D)),
                  _resident((D, D)),
                  _resident((1, D)),
                  _resident((D, 2 * D_FF)),
                  _resident((D_FF, D)),
                  _resident((1, D))],
        out_specs=row_spec,
        out_shape=jax.ShapeDtypeStruct((T, D), jnp.float32),
        scratch_shapes=[pltpu.VMEM((TAIL_ROWS, D), jnp.float32)],
        compiler_params=pltpu.CompilerParams(
            dimension_semantics=("arbitrary",), vmem_limit_bytes=V7X_VMEM_LIMIT_BYTES),
        name="tail",
    )(x2d, attn.reshape(T, D), u, vn, gates, gate_b[0].reshape(1, 2 * D), gm_ws[0], bs_cols,
      w_out[0].astype(bf16), norm_ffn_g, w_ffn_in[0].astype(bf16), w_ffn_out[0].astype(bf16),
      norm_final_g.reshape(1, D))
    return out.reshape(B, S, D)
```

```python
import math

import jax
import jax.numpy as jnp
import numpy as np
from jax import lax
from jax.experimental import pallas as pl
from jax.experimental.pallas import tpu as pltpu

D_MODEL = 1024
HEADS = 8
HEAD_DIM = 64
V_DIM = 128
ROT_DIM = 16
ROT_HALF = ROT_DIM // 2
ROPE_THETA = 500000.0
CHUNK = 64
GM_GROUPS = 8
GM_BLOCK = 128
D_FF = 2816
EPS = 1e-6
SUBLN_EPS = 1e-5
LAMBDA_INIT = 0.8 - 0.6 * math.exp(-0.3 * 0)
N_SECTIONS = 7

V7X_VMEM_LIMIT_BYTES = 58 * 1024 * 1024

PROJ_ROWS = 512
ATTN_BLOCK = 256
SCORE_LOOKAHEAD = 6
ACC_ROWS = V_DIM + 16
TAIL_ROWS = 512
FFN_CHUNKS = ((0, 1536), (1536, 1280))


def _rmsnorm(x, g, eps):
    return (x * lax.rsqrt(jnp.mean(x * x, axis=-1, keepdims=True) + eps)) * g


def _gelu(x):
    return 0.5 * x * (1.0 + lax.erf(x * np.float32(np.sqrt(0.5))))


def _rope_transposed(xt, cos, sin):
    pieces = []
    for g in range(D_MODEL // HEAD_DIM):
        base = g * HEAD_DIM
        x1 = xt[base:base + ROT_HALF]
        x2 = xt[base + ROT_HALF:base + ROT_DIM]
        pieces.append(x1 * cos - x2 * sin)
        pieces.append(x2 * cos + x1 * sin)
        pieces.append(xt[base + ROT_DIM:base + HEAD_DIM])
    return jnp.concatenate(pieces, axis=0)


def _proj_kernel(x_ref, pos_ref, invf_ref, g_ref, w_ref, gmg_ref, gb_ref, ws_ref, bs_ref,
                 qt_ref, k_ref, vt_ref, ga_ref, gmo_ref, u_ref, vn_ref):
    x = x_ref[...]
    h = _rmsnorm(x, g_ref[...], EPS).astype(jnp.bfloat16)

    ang = invf_ref[...] * pos_ref[...].astype(jnp.float32)
    cos = jnp.cos(ang)
    sin = jnp.sin(ang)

    def section(s):
        return jnp.dot(h, w_ref[:, s * D_MODEL:(s + 1) * D_MODEL],
                       preferred_element_type=jnp.float32)

    n_blk = PROJ_ROWS // ATTN_BLOCK

    qt = _rope_transposed(section(0).T, cos, sin) * np.float32(HEAD_DIM ** -0.5 * math.log2(math.e))
    qt = qt.astype(jnp.bfloat16)
    for c in range(n_blk):
        qt_ref[0, c] = qt[:, c * ATTN_BLOCK:(c + 1) * ATTN_BLOCK]

    kt = _rope_transposed(section(1).T, cos, sin)
    k_ref[...] = kt.T.astype(jnp.bfloat16)

    vt = section(2).T.astype(jnp.bfloat16)
    for c in range(n_blk):
        vt_ref[0, c] = vt[:, c * ATTN_BLOCK:(c + 1) * ATTN_BLOCK]

    u_ref[...] = _gelu(section(3))
    vn_ref[...] = _rmsnorm(_gelu(section(4)), gmg_ref[...], EPS).astype(jnp.bfloat16)
    gate_a = jax.nn.sigmoid(section(5) + gb_ref[:, 0:D_MODEL])
    ga_ref[...] = gate_a.astype(jnp.bfloat16)
    gate_b = jax.nn.sigmoid(section(6) + gb_ref[:, D_MODEL:2 * D_MODEL])

    n_gm = PROJ_ROWS // GM_BLOCK
    pi = lax.broadcasted_iota(jnp.int32, (GM_BLOCK, GM_BLOCK), 0) // CHUNK
    pj = lax.broadcasted_iota(jnp.int32, (GM_BLOCK, GM_BLOCK), 1) // CHUNK
    causal = pj <= pi
    for g in range(GM_GROUPS):
        cols = slice(g * GM_BLOCK, (g + 1) * GM_BLOCK)
        wm = jnp.where(causal, ws_ref[g], 0.0).astype(jnp.bfloat16)
        rhs = jnp.concatenate(
            [vn_ref[nb * GM_BLOCK:(nb + 1) * GM_BLOCK, cols] for nb in range(n_gm)], axis=1)
        mix = jnp.dot(wm, rhs, preferred_element_type=jnp.float32)
        for nb in range(n_gm):
            rows = slice(nb * GM_BLOCK, (nb + 1) * GM_BLOCK)
            mixed = mix[:, nb * GM_BLOCK:(nb + 1) * GM_BLOCK] + bs_ref[:, cols]
            gmo_ref[rows, cols] = (gate_b[rows, cols] * (u_ref[rows, cols] * mixed)).astype(jnp.bfloat16)


def _attn_kernel(qt_ref, k_ref, vt_ref, lam_ref, sg_ref, o_ref, m_ref, acc_ref):
    p_id = pl.program_id(1)
    tq = ATTN_BLOCK
    nq = vt_ref.shape[1]
    tile_a = p_id
    tile_b = nq - 1 - p_id
    tiles = (tile_a, tile_b)

    m_ref[0] = jnp.full(m_ref.shape[1:], -jnp.inf, jnp.float32)
    acc_ref[0] = jnp.zeros(acc_ref.shape[1:], jnp.float32)

    kr = lax.broadcasted_iota(jnp.int32, (tq, tq), 0) // CHUNK
    qc = lax.broadcasted_iota(jnp.int32, (tq, tq), 1) // CHUNK
    allowed = kr <= qc
    ones_rows = jnp.ones((ACC_ROWS - V_DIM, tq), jnp.bfloat16)

    slots = [(1, tile_b, tile_b, True, True)]
    for t in range(1, nq):
        is_b = t <= tile_b
        slots.append((is_b.astype(jnp.int32), jnp.where(is_b, tile_b, tile_a),
                      jnp.where(is_b, tile_b - t, t - (tile_b + 1)), False, False))
    slots.append((0, tile_a, tile_a, True, False))

    items = [(t, h, mp) for t in range(len(slots)) for h in range(HEADS) for mp in range(2)]

    def scores(t, h, mp):
        _, tile, blk, _, _ = slots[t]
        rows = pl.ds(pl.multiple_of(blk * tq, tq), tq)
        kb = k_ref[0, rows, h * V_DIM:(h + 1) * V_DIM]
        qh = qt_ref[0, tile, h * V_DIM + mp * HEAD_DIM:h * V_DIM + (mp + 1) * HEAD_DIM, :]
        z = jnp.zeros_like(qh)
        rhs = jnp.concatenate([qh, z] if mp == 0 else [z, qh], axis=0)
        return jnp.dot(kb, rhs, preferred_element_type=jnp.float32)

    pending = [scores(*items[i]) for i in range(SCORE_LOOKAHEAD)]
    for idx, (t, h, mp) in enumerate(items):
        sel, _, blk, masked, first = slots[t]
        sm = pending.pop(0)
        if idx + SCORE_LOOKAHEAD < len(items):
            pending.append(scores(*items[idx + SCORE_LOOKAHEAD]))
        vta = jnp.concatenate([vt_ref[0, blk, h * V_DIM:(h + 1) * V_DIM, :], ones_rows], axis=0)
        if masked:
            sm = jnp.where(allowed, sm, -jnp.inf)
        if first:
            m_new = jnp.max(sm, axis=0, keepdims=True)
        else:
            m_old = m_ref[sel, h, mp]
            m_new = jnp.maximum(m_old, jnp.max(sm, axis=0, keepdims=True))
            alpha = jnp.exp2(m_old - m_new)
        p = jnp.exp2(sm - m_new).astype(jnp.bfloat16)
        pv = jnp.dot(vta, p, preferred_element_type=jnp.float32)
        m_ref[sel, h, mp] = m_new
        if first:
            acc_ref[sel, h, mp] = pv
        else:
            acc_ref[sel, h, mp] = alpha * acc_ref[sel, h, mp] + pv

    lf = lam_ref[...]
    lam = (jnp.exp(jnp.sum(lf[0:1] * lf[1:2], axis=-1, keepdims=True))
           - jnp.exp(jnp.sum(lf[2:3] * lf[3:4], axis=-1, keepdims=True))
           + np.float32(LAMBDA_INIT))
    for sel in range(2):
        out_rows = pl.ds(pl.multiple_of(tiles[sel] * tq, tq), tq)
        for h in range(HEADS):
            ot = (acc_ref[sel, h, 0, 0:V_DIM] / acc_ref[sel, h, 0, V_DIM:V_DIM + 1]
                  - lam * (acc_ref[sel, h, 1, 0:V_DIM] / acc_ref[sel, h, 1, V_DIM:V_DIM + 1]))
            ot = ot * lax.rsqrt(jnp.mean(ot * ot, axis=0, keepdims=True) + SUBLN_EPS)
            o = (ot.T * sg_ref[...]) * np.float32(1.0 - LAMBDA_INIT)
            o_ref[0, out_rows, h * V_DIM:(h + 1) * V_DIM] = o.astype(o_ref.dtype)


def _tail_kernel(x_ref, attn_ref, ga_ref, gmo_ref, wo_ref, g2_ref, wfi_ref, wfo_ref, gf_ref, out_ref):
    merged = (ga_ref[...].astype(jnp.float32) * attn_ref[...].astype(jnp.float32)
              + gmo_ref[...].astype(jnp.float32))
    x1 = x_ref[...] + jnp.dot(merged.astype(jnp.bfloat16), wo_ref[...],
                              preferred_element_type=jnp.float32)

    h2 = _rmsnorm(x1, g2_ref[...], EPS).astype(jnp.bfloat16)
    ff = None
    for (c0, cw) in FFN_CHUNKS:
        a = jnp.dot(h2, wfi_ref[:, c0:c0 + cw], preferred_element_type=jnp.float32)
        b = jnp.dot(h2, wfi_ref[:, D_FF + c0:D_FF + c0 + cw], preferred_element_type=jnp.float32)
        act = (a * jax.nn.sigmoid(a) * b).astype(jnp.bfloat16)
        part = jnp.dot(act, wfo_ref[c0:c0 + cw, :], preferred_element_type=jnp.float32)
        ff = part if ff is None else ff + part
    x2 = x1 + ff
    out_ref[...] = _rmsnorm(x2, gf_ref[...], EPS)


def _resident(shape):
    return pl.BlockSpec(shape, lambda *_: (0,) * len(shape), pipeline_mode=pl.Buffered(1))


def kernel(x, positions, norm_mix_g, w_in, gate_b, lambdas, subln_g, gm_norm_g, gm_ws, gm_bs,
           w_out, norm_ffn_g, w_ffn_in, w_ffn_out, norm_final_g):
    B, S, D = x.shape
    T = B * S
    assert D == D_MODEL and S % PROJ_ROWS == 0 and S % ATTN_BLOCK == 0 and S % TAIL_ROWS == 0
    assert norm_mix_g.shape[0] == 1, "single-layer block"
    nq = S // ATTN_BLOCK
    bf16 = jnp.bfloat16

    x2d = x.reshape(T, D)
    pos_row = positions.reshape(1, T)
    inv_freq = ROPE_THETA ** (-np.arange(0, ROT_DIM, 2, dtype=np.float32) / ROT_DIM)
    invf_col = jnp.asarray(inv_freq, jnp.float32).reshape(ROT_HALF, 1)

    bs_cols = jnp.repeat(gm_bs[0].T, GM_BLOCK, axis=1)
    tiles_per_seq = S // PROJ_ROWS
    blk_per_tile = PROJ_ROWS // ATTN_BLOCK
    row_spec = pl.BlockSpec((PROJ_ROWS, D), lambda i: (i, 0))
    t_spec = pl.BlockSpec((1, blk_per_tile, D, ATTN_BLOCK),
                          lambda i: (i // tiles_per_seq, i % tiles_per_seq, 0, 0))
    qt, k, vt, gate_a, gm_gated = pl.pallas_call(
        _proj_kernel,
        grid=(T // PROJ_ROWS,),
        in_specs=[row_spec,
                  pl.BlockSpec((1, PROJ_ROWS), lambda i: (0, i)),
                  _resident((ROT_HALF, 1)),
                  _resident((1, D)),
                  _resident((D, N_SECTIONS * D)),
                  _resident((1, D)),
                  _resident((1, 2 * D)),
                  _resident((GM_GROUPS, GM_BLOCK, GM_BLOCK)),
                  _resident((GM_BLOCK, D))],
        out_specs=[t_spec, row_spec, t_spec, row_spec, row_spec],
        out_shape=[jax.ShapeDtypeStruct((B, nq, D, ATTN_BLOCK), bf16),
                   jax.ShapeDtypeStruct((T, D), bf16),
                   jax.ShapeDtypeStruct((B, nq, D, ATTN_BLOCK), bf16),
                   jax.ShapeDtypeStruct((T, D), bf16),
                   jax.ShapeDtypeStruct((T, D), bf16)],
        scratch_shapes=[pltpu.VMEM((PROJ_ROWS, D), jnp.float32),
                        pltpu.VMEM((PROJ_ROWS, D), bf16)],
        compiler_params=pltpu.CompilerParams(
            dimension_semantics=("arbitrary",), vmem_limit_bytes=V7X_VMEM_LIMIT_BYTES),
        name="proj",
    )(x2d, pos_row, invf_col, norm_mix_g, w_in[0].astype(bf16), gm_norm_g,
      gate_b[0].reshape(1, 2 * D), gm_ws[0], bs_cols)

    assert nq % 2 == 0
    attn = pl.pallas_call(
        _attn_kernel,
        grid=(B, nq // 2),
        in_specs=[pl.BlockSpec((1, nq, D, ATTN_BLOCK), lambda b, p: (b, 0, 0, 0)),
                  pl.BlockSpec((1, S, D), lambda b, p: (b, 0, 0)),
                  pl.BlockSpec((1, nq, D, ATTN_BLOCK), lambda b, p: (b, 0, 0, 0)),
                  _resident((4, HEAD_DIM)),
                  _resident((1, V_DIM))],
        out_specs=pl.BlockSpec((1, S, D), lambda b, p: (b, 0, 0)),
        out_shape=jax.ShapeDtypeStruct((B, S, D), bf16),
        scratch_shapes=[pltpu.VMEM((2, HEADS, 2, 1, ATTN_BLOCK), jnp.float32),
                        pltpu.VMEM((2, HEADS, 2, ACC_ROWS, ATTN_BLOCK), jnp.float32)],
        compiler_params=pltpu.CompilerParams(
            dimension_semantics=("arbitrary", "arbitrary"),
            vmem_limit_bytes=V7X_VMEM_LIMIT_BYTES),
        name="diff_attn",
    )(qt, k.reshape(B, S, D), vt, lambdas[0], subln_g)

    row_spec = pl.BlockSpec((TAIL_ROWS, D), lambda i: (i, 0))
    out = pl.pallas_call(
        _tail_kernel,
        grid=(T // TAIL_ROWS,),
        in_specs=[row_spec, row_spec, row_spec, row_spec,
                  _resident((D, D)),
                  _resident((1, D)),
                  _resident((D, 2 * D_FF)),
                  _resident((D_FF, D)),
                  _resident((1, D))],
        out_specs=row_spec,
        out_shape=jax.ShapeDtypeStruct((T, D), jnp.float32),
        compiler_params=pltpu.CompilerParams(
            dimension_semantics=("arbitrary",), vmem_limit_bytes=V7X_VMEM_LIMIT_BYTES),
        name="tail",
    )(x2d, attn.reshape(T, D), gate_a, gm_gated,
      w_out[0].astype(bf16), norm_ffn_g, w_ffn_in[0].astype(bf16), w_ffn_out[0].astype(bf16),
      norm_final_g.reshape(1, D))
    return out.reshape(B, S, D)
```

```python
import math

import jax
import jax.numpy as jnp
import numpy as np
from jax import lax
from jax.experimental import pallas as pl
from jax.experimental.pallas import tpu as pltpu

D_MODEL = 1024
HEADS = 8
HEAD_DIM = 64
V_DIM = 128
ROT_DIM = 16
ROT_HALF = ROT_DIM // 2
ROPE_THETA = 500000.0
CHUNK = 64
GM_GROUPS = 8
GM_BLOCK = 128
D_FF = 2816
EPS = 1e-6
SUBLN_EPS = 1e-5
LAMBDA_INIT = 0.8 - 0.6 * math.exp(-0.3 * 0)
N_SECTIONS = 7

V7X_VMEM_LIMIT_BYTES = 58 * 1024 * 1024

PROJ_ROWS = 512
PROJ_ORDER = ("vn", "u", "gb", "q", "gm", "k", "v", "ga")
ATTN_BLOCK = 256
MIN_FAST_DENOM = 2.0 ** -64
SCORE_LOOKAHEAD = 4
ACC_ROWS = V_DIM + 16
WEIGHT_CAST_COLS = 128
TAIL_ROWS = 512
FFN_CHUNKS = ((0, 1536), (1536, 1280))


def _rmsnorm(x, g, eps):
    return (x * lax.rsqrt(jnp.mean(x * x, axis=-1, keepdims=True) + eps)) * g


def _gelu(x):
    return 0.5 * x * (1.0 + lax.erf(x * np.float32(np.sqrt(0.5))))


def _group_sq_norms(xt):
    x = xt.astype(jnp.float32)
    x = x * x
    return jnp.concatenate([jnp.sum(x[g * HEAD_DIM:(g + 1) * HEAD_DIM], axis=0, keepdims=True)
                            for g in range(D_MODEL // HEAD_DIM)], axis=0)


def _rope_transposed(xt, cos, sin):
    pieces = []
    for g in range(D_MODEL // HEAD_DIM):
        base = g * HEAD_DIM
        x1 = xt[base:base + ROT_HALF]
        x2 = xt[base + ROT_HALF:base + ROT_DIM]
        pieces.append(x1 * cos - x2 * sin)
        pieces.append(x2 * cos + x1 * sin)
        pieces.append(xt[base + ROT_DIM:base + HEAD_DIM])
    return jnp.concatenate(pieces, axis=0)


def _proj_kernel(x_ref, pos_ref, invf_ref, g_ref, w_ref, gmg_ref, gb_ref, ws_ref, bs_ref,
                 wfi32_ref, wfo32_ref, wo32_ref,
                 qt_ref, k_ref, vt_ref, ga_ref, gmo_ref, wfi_ref, wfo_ref, wo_ref, qn_ref, kn_ref,
                 u_ref, vn_ref):
    x = x_ref[...]
    h = _rmsnorm(x, g_ref[...], EPS).astype(jnp.bfloat16)

    ang = invf_ref[...] * pos_ref[...].astype(jnp.float32)
    cos = jnp.cos(ang)
    sin = jnp.sin(ang)

    n_blk = PROJ_ROWS // ATTN_BLOCK
    n_gm = PROJ_ROWS // GM_BLOCK
    vals = {}

    def section(s):
        return jnp.dot(h, w_ref[:, s * D_MODEL:(s + 1) * D_MODEL],
                       preferred_element_type=jnp.float32)

    def do_q():
        qt = _rope_transposed(section(0).T, cos, sin) * np.float32(HEAD_DIM ** -0.5 * math.log2(math.e))
        qt = qt.astype(jnp.bfloat16)
        qn = _group_sq_norms(qt)
        for c in range(n_blk):
            qt_ref[0, c] = qt[:, c * ATTN_BLOCK:(c + 1) * ATTN_BLOCK]
            qn_ref[0, c] = qn[:, c * ATTN_BLOCK:(c + 1) * ATTN_BLOCK]

    def do_k():
        kt = _rope_transposed(section(1).T, cos, sin)
        kn = _group_sq_norms(kt.astype(jnp.bfloat16))
        for c in range(n_blk):
            kn_ref[0, c] = kn[:, c * ATTN_BLOCK:(c + 1) * ATTN_BLOCK]
        k_ref[...] = kt.T.astype(jnp.bfloat16)

    def do_v():
        vt = section(2).T.astype(jnp.bfloat16)
        for c in range(n_blk):
            vt_ref[0, c] = vt[:, c * ATTN_BLOCK:(c + 1) * ATTN_BLOCK]

    def do_u():
        u_ref[...] = _gelu(section(3))

    def do_vn():
        vn_ref[...] = _rmsnorm(_gelu(section(4)), gmg_ref[...], EPS).astype(jnp.bfloat16)

    def do_ga():
        ga_ref[...] = jax.nn.sigmoid(section(5) + gb_ref[:, 0:D_MODEL]).astype(jnp.bfloat16)

    def do_gb():
        vals["gate_b"] = jax.nn.sigmoid(section(6) + gb_ref[:, D_MODEL:2 * D_MODEL])

    def do_gm():
        gate_b = vals["gate_b"]
        pi = lax.broadcasted_iota(jnp.int32, (GM_BLOCK, GM_BLOCK), 0) // CHUNK
        pj = lax.broadcasted_iota(jnp.int32, (GM_BLOCK, GM_BLOCK), 1) // CHUNK
        causal = pj <= pi
        for g in range(GM_GROUPS):
            cols = slice(g * GM_BLOCK, (g + 1) * GM_BLOCK)
            wm = jnp.where(causal, ws_ref[g], 0.0).astype(jnp.bfloat16)
            rhs = jnp.concatenate(
                [vn_ref[nb * GM_BLOCK:(nb + 1) * GM_BLOCK, cols] for nb in range(n_gm)], axis=1)
            mix = jnp.dot(wm, rhs, preferred_element_type=jnp.float32)
            for nb in range(n_gm):
                rows = slice(nb * GM_BLOCK, (nb + 1) * GM_BLOCK)
                mixed = mix[:, nb * GM_BLOCK:(nb + 1) * GM_BLOCK] + bs_ref[:, cols]
                gmo_ref[rows, cols] = (gate_b[rows, cols] * (u_ref[rows, cols] * mixed)).astype(jnp.bfloat16)

    jobs = dict(q=do_q, k=do_k, v=do_v, u=do_u, vn=do_vn, ga=do_ga, gb=do_gb, gm=do_gm)
    for name in PROJ_ORDER:
        jobs[name]()

    step = pl.program_id(0)
    for (first, n_blocks), src_ref, dst_ref in zip(_cast_schedule(), (wfi32_ref, wfo32_ref, wo32_ref),
                                                   (wfi_ref, wfo_ref, wo_ref)):
        @pl.when((step >= first) & (step < first + n_blocks))
        def _():
            dst_ref[...] = src_ref[...].astype(jnp.bfloat16)


def _attn_kernel(qt_ref, k_ref, vt_ref, qn_ref, kn_ref, lam_ref, sg_ref, o_ref, m_ref, acc_ref):
    p_id = pl.program_id(1)
    tq = ATTN_BLOCK
    nq = vt_ref.shape[1]
    tile_a = p_id
    tile_b = nq - 1 - p_id
    tiles = (tile_a, tile_b)

    kr = lax.broadcasted_iota(jnp.int32, (tq, tq), 0) // CHUNK
    qc = lax.broadcasted_iota(jnp.int32, (tq, tq), 1) // CHUNK
    allowed = kr <= qc
    ones_rows = jnp.ones((ACC_ROWS - V_DIM, tq), jnp.bfloat16)

    slots = [(1, tile_b, tile_b, True, True)]
    for t in range(1, nq):
        is_b = t <= tile_b
        slots.append((is_b.astype(jnp.int32), jnp.where(is_b, tile_b, tile_a),
                      jnp.where(is_b, tile_b - t, t - (tile_b + 1)), False, False))
    slots.append((0, tile_a, tile_a, True, False))

    items = [(t, h, mp) for t in range(len(slots)) for h in range(HEADS) for mp in range(2)]

    def scores(t, h, mp):
        _, tile, blk, _, _ = slots[t]
        rows = pl.ds(pl.multiple_of(blk * tq, tq), tq)
        kb = k_ref[0, rows, h * V_DIM:(h + 1) * V_DIM]
        qh = qt_ref[0, tile, h * V_DIM + mp * HEAD_DIM:h * V_DIM + (mp + 1) * HEAD_DIM, :]
        z = jnp.zeros_like(qh)
        rhs = jnp.concatenate([qh, z] if mp == 0 else [z, qh], axis=0)
        return jnp.dot(kb, rhs, preferred_element_type=jnp.float32)

    def fold_blocks(bounds):
        acc_ref[0] = jnp.zeros(acc_ref.shape[1:], jnp.float32)
        if bounds is None:
            m_ref[0] = jnp.full(m_ref.shape[1:], -jnp.inf, jnp.float32)
        pending = [scores(*items[i]) for i in range(SCORE_LOOKAHEAD)]
        for idx, (t, h, mp) in enumerate(items):
            sel, _, blk, masked, first = slots[t]
            sm = pending.pop(0)
            if idx + SCORE_LOOKAHEAD < len(items):
                pending.append(scores(*items[idx + SCORE_LOOKAHEAD]))
            vta = jnp.concatenate([vt_ref[0, blk, h * V_DIM:(h + 1) * V_DIM, :], ones_rows], axis=0)
            if masked:
                sm = jnp.where(allowed, sm, -jnp.inf)
            if bounds is not None:
                row = slice(2 * h + mp, 2 * h + mp + 1)
                if isinstance(sel, int):
                    offset = bounds[sel][row]
                else:
                    offset = jnp.where(sel == 1, bounds[1][row], bounds[0][row])
                alpha = None
            elif first:
                offset = jnp.max(sm, axis=0, keepdims=True)
                alpha = None
            else:
                m_old = m_ref[sel, h, mp]
                offset = jnp.maximum(m_old, jnp.max(sm, axis=0, keepdims=True))
                alpha = jnp.exp2(m_old - offset)
            p = jnp.exp2(sm - offset).astype(jnp.bfloat16)
            pv = jnp.dot(vta, p, preferred_element_type=jnp.float32)
            if bounds is None:
                m_ref[sel, h, mp] = offset
            if first:
                acc_ref[sel, h, mp] = pv
            elif alpha is None:
                acc_ref[sel, h, mp] = acc_ref[sel, h, mp] + pv
            else:
                acc_ref[sel, h, mp] = alpha * acc_ref[sel, h, mp] + pv

    kmax = jnp.max(jnp.max(kn_ref[0], axis=0), axis=-1, keepdims=True)
    fold_blocks([jnp.sqrt(qn_ref[0, tiles[sel]] * kmax) * np.float32(1.0 + 2.0 ** -9) for sel in range(2)])

    denom = acc_ref[0, 0, 0, V_DIM:V_DIM + 1, :]
    for sel in range(2):
        for h in range(HEADS):
            for mp in range(2):
                denom = jnp.minimum(denom, acc_ref[sel, h, mp, V_DIM:V_DIM + 1, :])
    denom_min = jnp.min(denom, axis=-1, keepdims=True)[0, 0]

    @pl.when(jnp.logical_not(denom_min >= np.float32(MIN_FAST_DENOM)))
    def _():
        fold_blocks(None)

    lf = lam_ref[...]
    lam = (jnp.exp(jnp.sum(lf[0:1] * lf[1:2], axis=-1, keepdims=True))
           - jnp.exp(jnp.sum(lf[2:3] * lf[3:4], axis=-1, keepdims=True))
           + np.float32(LAMBDA_INIT))
    for sel in range(2):
        out_rows = pl.ds(pl.multiple_of(tiles[sel] * tq, tq), tq)
        for h in range(HEADS):
            ot = (acc_ref[sel, h, 0, 0:V_DIM] / acc_ref[sel, h, 0, V_DIM:V_DIM + 1]
                  - lam * (acc_ref[sel, h, 1, 0:V_DIM] / acc_ref[sel, h, 1, V_DIM:V_DIM + 1]))
            ot = ot * lax.rsqrt(jnp.mean(ot * ot, axis=0, keepdims=True) + SUBLN_EPS)
            o = (ot.T * sg_ref[...]) * np.float32(1.0 - LAMBDA_INIT)
            o_ref[0, out_rows, h * V_DIM:(h + 1) * V_DIM] = o.astype(o_ref.dtype)


def _tail_kernel(x_ref, attn_ref, ga_ref, gmo_ref, wo_ref, g2_ref, wfi_ref, wfo_ref, gf_ref, out_ref):
    merged = (ga_ref[...].astype(jnp.float32) * attn_ref[...].astype(jnp.float32)
              + gmo_ref[...].astype(jnp.float32))
    x1 = x_ref[...] + jnp.dot(merged.astype(jnp.bfloat16), wo_ref[...],
                              preferred_element_type=jnp.float32)

    h2 = _rmsnorm(x1, g2_ref[...], EPS).astype(jnp.bfloat16)
    ff = None
    for (c0, cw) in FFN_CHUNKS:
        a = jnp.dot(h2, wfi_ref[:, c0:c0 + cw], preferred_element_type=jnp.float32)
        b = jnp.dot(h2, wfi_ref[:, D_FF + c0:D_FF + c0 + cw], preferred_element_type=jnp.float32)
        act = (a * jax.nn.sigmoid(a) * b).astype(jnp.bfloat16)
        part = jnp.dot(act, wfo_ref[c0:c0 + cw, :], preferred_element_type=jnp.float32)
        ff = part if ff is None else ff + part
    x2 = x1 + ff
    out_ref[...] = _rmsnorm(x2, gf_ref[...], EPS)


def _resident(shape):
    return pl.BlockSpec(shape, lambda *_: (0,) * len(shape), pipeline_mode=pl.Buffered(1))


def _cast_schedule():
    counts = (2 * D_FF // WEIGHT_CAST_COLS, D_MODEL // WEIGHT_CAST_COLS, D_MODEL // WEIGHT_CAST_COLS)
    firsts = (0, counts[0], counts[0] + counts[1])
    return tuple(zip(firsts, counts))


def _col_block(rows, first, n_blocks):
    return pl.BlockSpec((rows, WEIGHT_CAST_COLS), lambda i: (0, jnp.clip(i - first, 0, n_blocks - 1)))


def _run_proj(x2d, pos_row, B, S, norm_mix_g, w_in, gm_norm_g, gate_b, gm_ws, gm_bs, w_out, w_ffn_in, w_ffn_out):
    T, D = x2d.shape
    bf16 = jnp.bfloat16
    nq = S // ATTN_BLOCK
    n_steps = T // PROJ_ROWS
    inv_freq = ROPE_THETA ** (-np.arange(0, ROT_DIM, 2, dtype=np.float32) / ROT_DIM)
    invf_col = jnp.asarray(inv_freq, jnp.float32).reshape(ROT_HALF, 1)
    bs_cols = jnp.repeat(gm_bs.T, GM_BLOCK, axis=1)
    sched = _cast_schedule()
    assert sched[-1][0] + sched[-1][1] <= n_steps
    cast_specs = [_col_block(w.shape[0], first, n) for w, (first, n) in zip((w_ffn_in, w_ffn_out, w_out), sched)]

    tiles_per_seq = S // PROJ_ROWS
    blk_per_tile = PROJ_ROWS // ATTN_BLOCK
    row_spec = pl.BlockSpec((PROJ_ROWS, D), lambda i: (i, 0))
    t_spec = pl.BlockSpec((1, blk_per_tile, D, ATTN_BLOCK),
                          lambda i: (i // tiles_per_seq, i % tiles_per_seq, 0, 0))
    n_groups = D // HEAD_DIM
    n_spec = pl.BlockSpec((1, blk_per_tile, n_groups, ATTN_BLOCK),
                          lambda i: (i // tiles_per_seq, i % tiles_per_seq, 0, 0))
    return pl.pallas_call(
        _proj_kernel,
        grid=(n_steps,),
        in_specs=[row_spec,
                  pl.BlockSpec((1, PROJ_ROWS), lambda i: (0, i)),
                  _resident((ROT_HALF, 1)),
                  _resident((1, D)),
                  _resident((D, N_SECTIONS * D)),
                  _resident((1, D)),
                  _resident((1, 2 * D)),
                  _resident((GM_GROUPS, GM_BLOCK, GM_BLOCK)),
                  _resident((GM_BLOCK, D))] + cast_specs,
        out_specs=[t_spec, row_spec, t_spec, row_spec, row_spec] + cast_specs + [n_spec, n_spec],
        out_shape=[jax.ShapeDtypeStruct((B, nq, D, ATTN_BLOCK), bf16),
                   jax.ShapeDtypeStruct((T, D), bf16),
                   jax.ShapeDtypeStruct((B, nq, D, ATTN_BLOCK), bf16),
                   jax.ShapeDtypeStruct((T, D), bf16),
                   jax.ShapeDtypeStruct((T, D), bf16),
                   jax.ShapeDtypeStruct(w_ffn_in.shape, bf16),
                   jax.ShapeDtypeStruct(w_ffn_out.shape, bf16),
                   jax.ShapeDtypeStruct(w_out.shape, bf16),
                   jax.ShapeDtypeStruct((B, nq, n_groups, ATTN_BLOCK), jnp.float32),
                   jax.ShapeDtypeStruct((B, nq, n_groups, ATTN_BLOCK), jnp.float32)],
        scratch_shapes=[pltpu.VMEM((PROJ_ROWS, D), jnp.float32),
                        pltpu.VMEM((PROJ_ROWS, D), bf16)],
        compiler_params=pltpu.CompilerParams(
            dimension_semantics=("arbitrary",), vmem_limit_bytes=V7X_VMEM_LIMIT_BYTES),
        name="proj",
    )(x2d, pos_row, invf_col, norm_mix_g, w_in.astype(bf16), gm_norm_g,
      gate_b.reshape(1, 2 * D), gm_ws, bs_cols, w_ffn_in, w_ffn_out, w_out)


def _run_attn(qt, k, vt, qn, kn, lambdas, subln_g):
    B, nq, D, _ = qt.shape
    S = nq * ATTN_BLOCK
    assert nq % 2 == 0
    return pl.pallas_call(
        _attn_kernel,
        grid=(B, nq // 2),
        in_specs=[pl.BlockSpec((1, nq, D, ATTN_BLOCK), lambda b, p: (b, 0, 0, 0)),
                  pl.BlockSpec((1, S, D), lambda b, p: (b, 0, 0)),
                  pl.BlockSpec((1, nq, D, ATTN_BLOCK), lambda b, p: (b, 0, 0, 0)),
                  pl.BlockSpec((1, nq, D // HEAD_DIM, ATTN_BLOCK), lambda b, p: (b, 0, 0, 0)),
                  pl.BlockSpec((1, nq, D // HEAD_DIM, ATTN_BLOCK), lambda b, p: (b, 0, 0, 0)),
                  _resident((4, HEAD_DIM)),
                  _resident((1, V_DIM))],
        out_specs=pl.BlockSpec((1, S, D), lambda b, p: (b, 0, 0)),
        out_shape=jax.ShapeDtypeStruct((B, S, D), jnp.bfloat16),
        scratch_shapes=[pltpu.VMEM((2, HEADS, 2, 1, ATTN_BLOCK), jnp.float32),
                        pltpu.VMEM((2, HEADS, 2, ACC_ROWS, ATTN_BLOCK), jnp.float32)],
        compiler_params=pltpu.CompilerParams(
            dimension_semantics=("arbitrary", "arbitrary"),
            vmem_limit_bytes=V7X_VMEM_LIMIT_BYTES),
        name="diff_attn",
    )(qt, k.reshape(B, S, D), vt, qn, kn, lambdas, subln_g)


def _run_tail(x2d, attn2d, gate_a, gm_gated, wo, norm_ffn_g, wfi, wfo, norm_final_g):
    T, D = x2d.shape
    row_spec = pl.BlockSpec((TAIL_ROWS, D), lambda i: (i, 0))
    return pl.pallas_call(
        _tail_kernel,
        grid=(T // TAIL_ROWS,),
        in_specs=[row_spec, row_spec, row_spec, row_spec,
                  _resident((D, D)),
                  _resident((1, D)),
                  _resident((D, 2 * D_FF)),
                  _resident((D_FF, D)),
                  _resident((1, D))],
        out_specs=row_spec,
        out_shape=jax.ShapeDtypeStruct((T, D), jnp.float32),
        compiler_params=pltpu.CompilerParams(
            dimension_semantics=("arbitrary",), vmem_limit_bytes=V7X_VMEM_LIMIT_BYTES),
        name="tail",
    )(x2d, attn2d, gate_a, gm_gated, wo, norm_ffn_g, wfi, wfo, norm_final_g.reshape(1, D))


def kernel(x, positions, norm_mix_g, w_in, gate_b, lambdas, subln_g, gm_norm_g, gm_ws, gm_bs,
           w_out, norm_ffn_g, w_ffn_in, w_ffn_out, norm_final_g):
    B, S, D = x.shape
    T = B * S
    assert D == D_MODEL and S % PROJ_ROWS == 0 and S % ATTN_BLOCK == 0 and S % TAIL_ROWS == 0
    assert norm_mix_g.shape[0] == 1, "single-layer block"
    x2d = x.reshape(T, D)
    qt, k, vt, gate_a, gm_gated, wfi, wfo, wo, qn, kn = _run_proj(
        x2d, positions.reshape(1, T), B, S, norm_mix_g, w_in[0], gm_norm_g, gate_b[0], gm_ws[0], gm_bs[0],
        w_out[0], w_ffn_in[0], w_ffn_out[0])
    attn = _run_attn(qt, k, vt, qn, kn, lambdas[0], subln_g)
    out = _run_tail(x2d, attn.reshape(T, D), gate_a, gm_gated, wo, norm_ffn_g, wfi, wfo, norm_final_g)
    return out.reshape(B, S, D)
```

```python
import math

import jax
import jax.numpy as jnp
import numpy as np
from jax import lax
from jax.experimental import pallas as pl
from jax.experimental.pallas import tpu as pltpu

D_MODEL = 1024
HEADS = 8
HEAD_DIM = 64
V_DIM = 128
ROT_DIM = 16
ROT_HALF = ROT_DIM // 2
ROPE_THETA = 500000.0
CHUNK = 64
GM_GROUPS = 8
GM_BLOCK = 128
D_FF = 2816
EPS = 1e-6
SUBLN_EPS = 1e-5
LAMBDA_INIT = 0.8 - 0.6 * math.exp(-0.3 * 0)
N_SECTIONS = 7

V7X_VMEM_LIMIT_BYTES = 58 * 1024 * 1024

PROJ_ROWS = 512
PROJ_ORDER = ("vn", "u", "gb", "q", "gm", "k", "v", "ga")
ATTN_BLOCK = 256
MAX_FAST_OVERSHOOT = 64.0
BOUND_SLACK = 1.0 + 2.0 ** -9
SCORE_LOOKAHEAD = 4
BF16_ROWS_PER_VREG = 16
ACC_ROWS = V_DIM + BF16_ROWS_PER_VREG
WEIGHT_CAST_COLS = 128
TAIL_ROWS = 512
FFN_CHUNKS = ((0, 1536), (1536, 1280))


def _rmsnorm(x, g, eps):
    return (x * lax.rsqrt(jnp.mean(x * x, axis=-1, keepdims=True) + eps)) * g


def _gelu(x):
    return 0.5 * x * (1.0 + lax.erf(x * np.float32(np.sqrt(0.5))))


def _group_sums(x):
    return jnp.concatenate([jnp.sum(x[g * HEAD_DIM:(g + 1) * HEAD_DIM], axis=0, keepdims=True)
                            for g in range(D_MODEL // HEAD_DIM)], axis=0)


def _rope_transposed(xt, cos, sin):
    pieces = []
    for g in range(D_MODEL // HEAD_DIM):
        base = g * HEAD_DIM
        x1 = xt[base:base + ROT_HALF]
        x2 = xt[base + ROT_HALF:base + ROT_DIM]
        pieces.append(x1 * cos - x2 * sin)
        pieces.append(x2 * cos + x1 * sin)
        pieces.append(xt[base + ROT_DIM:base + HEAD_DIM])
    return jnp.concatenate(pieces, axis=0)


def _proj_kernel(x_ref, pos_ref, invf_ref, g_ref, w_ref, gmg_ref, gb_ref, ws_ref, bs_ref,
                 wfi32_ref, wfo32_ref, wo32_ref,
                 qt_ref, k_ref, vt_ref, ga_ref, gmo_ref, wfi_ref, wfo_ref, wo_ref, qn_ref, kn_ref, qk_ref,
                 u_ref, vn_ref):
    x = x_ref[...]
    h = _rmsnorm(x, g_ref[...], EPS).astype(jnp.bfloat16)

    ang = invf_ref[...] * pos_ref[...].astype(jnp.float32)
    cos = jnp.cos(ang)
    sin = jnp.sin(ang)

    n_blk = PROJ_ROWS // ATTN_BLOCK
    n_gm = PROJ_ROWS // GM_BLOCK
    vals = {}

    def section(s):
        return jnp.dot(h, w_ref[:, s * D_MODEL:(s + 1) * D_MODEL],
                       preferred_element_type=jnp.float32)

    def score_stats():
        if "q" in vals and "k" in vals:
            stats = (_group_sums(vals["q"] * vals["q"]), _group_sums(vals["k"] * vals["k"]),
                     _group_sums(vals["q"] * vals["k"]))
            for ref, stat in zip((qn_ref, kn_ref, qk_ref), stats):
                for c in range(n_blk):
                    ref[0, c] = stat[:, c * ATTN_BLOCK:(c + 1) * ATTN_BLOCK]

    def do_q():
        qt = _rope_transposed(section(0).T, cos, sin) * np.float32(HEAD_DIM ** -0.5 * math.log2(math.e))
        qt = qt.astype(jnp.bfloat16)
        vals["q"] = qt.astype(jnp.float32)
        for c in range(n_blk):
            qt_ref[0, c] = qt[:, c * ATTN_BLOCK:(c + 1) * ATTN_BLOCK]
        score_stats()

    def do_k():
        kt = _rope_transposed(section(1).T, cos, sin)
        vals["k"] = kt.astype(jnp.bfloat16).astype(jnp.float32)
        k_ref[...] = kt.T.astype(jnp.bfloat16)
        score_stats()

    def do_v():
        vt = section(2).T.astype(jnp.bfloat16)
        for c in range(n_blk):
            vt_ref[0, c] = vt[:, c * ATTN_BLOCK:(c + 1) * ATTN_BLOCK]

    def do_u():
        u_ref[...] = _gelu(section(3))

    def do_vn():
        vn_ref[...] = _rmsnorm(_gelu(section(4)), gmg_ref[...], EPS).astype(jnp.bfloat16)

    def do_ga():
        ga_ref[...] = jax.nn.sigmoid(section(5) + gb_ref[:, 0:D_MODEL]).astype(jnp.bfloat16)

    def do_gb():
        vals["gate_b"] = jax.nn.sigmoid(section(6) + gb_ref[:, D_MODEL:2 * D_MODEL])

    def do_gm():
        gate_b = vals["gate_b"]
        pi = lax.broadcasted_iota(jnp.int32, (GM_BLOCK, GM_BLOCK), 0) // CHUNK
        pj = lax.broadcasted_iota(jnp.int32, (GM_BLOCK, GM_BLOCK), 1) // CHUNK
        causal = pj <= pi
        for g in range(GM_GROUPS):
            cols = slice(g * GM_BLOCK, (g + 1) * GM_BLOCK)
            wm = jnp.where(causal, ws_ref[g], 0.0).astype(jnp.bfloat16)
            rhs = jnp.concatenate(
                [vn_ref[nb * GM_BLOCK:(nb + 1) * GM_BLOCK, cols] for nb in range(n_gm)], axis=1)
            mix = jnp.dot(wm, rhs, preferred_element_type=jnp.float32)
            for nb in range(n_gm):
                rows = slice(nb * GM_BLOCK, (nb + 1) * GM_BLOCK)
                mixed = mix[:, nb * GM_BLOCK:(nb + 1) * GM_BLOCK] + bs_ref[:, cols]
                gmo_ref[rows, cols] = (gate_b[rows, cols] * (u_ref[rows, cols] * mixed)).astype(jnp.bfloat16)

    jobs = dict(q=do_q, k=do_k, v=do_v, u=do_u, vn=do_vn, ga=do_ga, gb=do_gb, gm=do_gm)
    for name in PROJ_ORDER:
        jobs[name]()

    step = pl.program_id(0)
    for (first, n_blocks), src_ref, dst_ref in zip(_cast_schedule(), (wfi32_ref, wfo32_ref, wo32_ref),
                                                   (wfi_ref, wfo_ref, wo_ref)):
        @pl.when((step >= first) & (step < first + n_blocks))
        def _():
            dst_ref[...] = src_ref[...].astype(jnp.bfloat16)


def _attn_kernel(qt_ref, k_ref, vt_ref, qn_ref, kn_ref, qk_ref, lam_ref, sg_ref, o_ref,
                 m_ref, acc_ref, kmax_ref, fast_ref):
    p_id = pl.program_id(1)
    tq = ATTN_BLOCK
    nq = vt_ref.shape[1]
    tile_a = p_id
    tile_b = nq - 1 - p_id
    tiles = (tile_a, tile_b)

    kr = lax.broadcasted_iota(jnp.int32, (tq, tq), 0) // CHUNK
    qc = lax.broadcasted_iota(jnp.int32, (tq, tq), 1) // CHUNK
    allowed = kr <= qc
    ones_rows = jnp.ones((ACC_ROWS - V_DIM, tq), jnp.bfloat16)

    slots = [(1, tile_b, tile_b, True, True)]
    for t in range(1, nq):
        is_b = t <= tile_b
        slots.append((is_b.astype(jnp.int32), jnp.where(is_b, tile_b, tile_a),
                      jnp.where(is_b, tile_b - t, t - (tile_b + 1)), False, False))
    slots.append((0, tile_a, tile_a, True, False))

    items = [(t, h, mp) for t in range(len(slots)) for h in range(HEADS) for mp in range(2)]

    def scores(t, h, mp):
        _, tile, blk, _, _ = slots[t]
        rows = pl.ds(pl.multiple_of(blk * tq, tq), tq)
        kb = k_ref[0, rows, h * V_DIM:(h + 1) * V_DIM]
        qh = qt_ref[0, tile, h * V_DIM + mp * HEAD_DIM:h * V_DIM + (mp + 1) * HEAD_DIM, :]
        z = jnp.zeros_like(qh)
        rhs = jnp.concatenate([qh, z] if mp == 0 else [z, qh], axis=0)
        return jnp.dot(kb, rhs, preferred_element_type=jnp.float32)

    def fold_blocks(bounds):
        acc_ref[0] = jnp.zeros(acc_ref.shape[1:], jnp.float32)
        if bounds is None:
            m_ref[0] = jnp.full(m_ref.shape[1:], -jnp.inf, jnp.float32)
        pending = [scores(*items[i]) for i in range(SCORE_LOOKAHEAD)]
        for idx, (t, h, mp) in enumerate(items):
            sel, _, blk, masked, first = slots[t]
            sm = pending.pop(0)
            if idx + SCORE_LOOKAHEAD < len(items):
                pending.append(scores(*items[idx + SCORE_LOOKAHEAD]))
            vta = jnp.concatenate([vt_ref[0, blk, h * V_DIM:(h + 1) * V_DIM, :], ones_rows], axis=0)
            if masked:
                sm = jnp.where(allowed, sm, -jnp.inf)
            if bounds is not None:
                row = slice(2 * h + mp, 2 * h + mp + 1)
                if isinstance(sel, int):
                    offset = bounds[sel][row]
                else:
                    offset = jnp.where(sel == 1, bounds[1][row], bounds[0][row])
                alpha = None
            elif first:
                offset = jnp.max(sm, axis=0, keepdims=True)
                alpha = None
            else:
                m_old = m_ref[sel, h, mp]
                offset = jnp.maximum(m_old, jnp.max(sm, axis=0, keepdims=True))
                alpha = jnp.exp2(m_old - offset)
            p = jnp.exp2(sm - offset).astype(jnp.bfloat16)
            pv = jnp.dot(vta, p, preferred_element_type=jnp.float32)
            if bounds is None:
                m_ref[sel, h, mp] = offset
            if first:
                acc_ref[sel, h, mp] = pv
            elif alpha is None:
                acc_ref[sel, h, mp] = acc_ref[sel, h, mp] + pv
            else:
                acc_ref[sel, h, mp] = alpha * acc_ref[sel, h, mp] + pv

    @pl.when(p_id == 0)
    def _():
        kmax = jnp.max(jnp.max(kn_ref[0], axis=0), axis=-1, keepdims=True)
        kmax_ref[...] = jnp.broadcast_to(kmax, kmax_ref.shape)
        for pp in range(nq // 2):
            over = None
            for tile in (pp, nq - 1 - pp):
                o = jnp.sqrt(qn_ref[0, tile] * kmax) * np.float32(BOUND_SLACK) - qk_ref[0, tile]
                over = o if over is None else jnp.maximum(over, o)
            margin = jnp.max(jnp.max(over, axis=0, keepdims=True), axis=-1, keepdims=True)[0, 0]
            fast_ref[pp] = (margin <= np.float32(MAX_FAST_OVERSHOOT)).astype(jnp.int32)

    bounds = [jnp.sqrt(qn_ref[0, tiles[sel]] * kmax_ref[...]) * np.float32(BOUND_SLACK) for sel in range(2)]
    fast_ok = fast_ref[p_id] == 1

    def write_outputs():
        lf = lam_ref[...]
        lam = (jnp.exp(jnp.sum(lf[0:1] * lf[1:2], axis=-1, keepdims=True))
               - jnp.exp(jnp.sum(lf[2:3] * lf[3:4], axis=-1, keepdims=True))
               + np.float32(LAMBDA_INIT))
        for sel in range(2):
            out_rows = pl.ds(pl.multiple_of(tiles[sel] * tq, tq), tq)
            for h in range(HEADS):
                ot = (acc_ref[sel, h, 0, 0:V_DIM] / acc_ref[sel, h, 0, V_DIM:V_DIM + 1]
                      - lam * (acc_ref[sel, h, 1, 0:V_DIM] / acc_ref[sel, h, 1, V_DIM:V_DIM + 1]))
                ot = ot * lax.rsqrt(jnp.mean(ot * ot, axis=0, keepdims=True) + SUBLN_EPS)
                o = (ot.T * sg_ref[...]) * np.float32(1.0 - LAMBDA_INIT)
                o_ref[0, out_rows, h * V_DIM:(h + 1) * V_DIM] = o.astype(o_ref.dtype)

    @pl.when(fast_ok)
    def _():
        fold_blocks(bounds)
        write_outputs()

    @pl.when(jnp.logical_not(fast_ok))
    def _():
        fold_blocks(None)
        write_outputs()


def _tail_kernel(x_ref, attn_ref, ga_ref, gmo_ref, wo_ref, g2_ref, wfi_ref, wfo_ref, gf_ref, out_ref):
    merged = (ga_ref[...].astype(jnp.float32) * attn_ref[...].astype(jnp.float32)
              + gmo_ref[...].astype(jnp.float32))
    x1 = x_ref[...] + jnp.dot(merged.astype(jnp.bfloat16), wo_ref[...],
                              preferred_element_type=jnp.float32)

    h2 = _rmsnorm(x1, g2_ref[...], EPS).astype(jnp.bfloat16)
    ff = None
    for (c0, cw) in FFN_CHUNKS:
        a = jnp.dot(h2, wfi_ref[:, c0:c0 + cw], preferred_element_type=jnp.float32)
        b = jnp.dot(h2, wfi_ref[:, D_FF + c0:D_FF + c0 + cw], preferred_element_type=jnp.float32)
        act = (a * jax.nn.sigmoid(a) * b).astype(jnp.bfloat16)
        part = jnp.dot(act, wfo_ref[c0:c0 + cw, :], preferred_element_type=jnp.float32)
        ff = part if ff is None else ff + part
    x2 = x1 + ff
    out_ref[...] = _rmsnorm(x2, gf_ref[...], EPS)


def _resident(shape):
    return pl.BlockSpec(shape, lambda *_: (0,) * len(shape), pipeline_mode=pl.Buffered(1))


def _cast_schedule():
    counts = (2 * D_FF // WEIGHT_CAST_COLS, D_MODEL // WEIGHT_CAST_COLS, D_MODEL // WEIGHT_CAST_COLS)
    firsts = (0, counts[0], counts[0] + counts[1])
    return tuple(zip(firsts, counts))


def _col_block(rows, first, n_blocks):
    return pl.BlockSpec((rows, WEIGHT_CAST_COLS), lambda i: (0, jnp.clip(i - first, 0, n_blocks - 1)))


def _run_proj(x2d, pos_row, B, S, norm_mix_g, w_in, gm_norm_g, gate_b, gm_ws, gm_bs, w_out, w_ffn_in, w_ffn_out):
    T, D = x2d.shape
    bf16 = jnp.bfloat16
    nq = S // ATTN_BLOCK
    n_steps = T // PROJ_ROWS
    inv_freq = ROPE_THETA ** (-np.arange(0, ROT_DIM, 2, dtype=np.float32) / ROT_DIM)
    invf_col = jnp.asarray(inv_freq, jnp.float32).reshape(ROT_HALF, 1)
    bs_cols = jnp.repeat(gm_bs.T, GM_BLOCK, axis=1)
    sched = _cast_schedule()
    assert sched[-1][0] + sched[-1][1] <= n_steps
    cast_specs = [_col_block(w.shape[0], first, n) for w, (first, n) in zip((w_ffn_in, w_ffn_out, w_out), sched)]

    tiles_per_seq = S // PROJ_ROWS
    blk_per_tile = PROJ_ROWS // ATTN_BLOCK
    row_spec = pl.BlockSpec((PROJ_ROWS, D), lambda i: (i, 0))
    t_spec = pl.BlockSpec((1, blk_per_tile, D, ATTN_BLOCK),
                          lambda i: (i // tiles_per_seq, i % tiles_per_seq, 0, 0))
    n_groups = D // HEAD_DIM
    n_spec = pl.BlockSpec((1, blk_per_tile, n_groups, ATTN_BLOCK),
                          lambda i: (i // tiles_per_seq, i % tiles_per_seq, 0, 0))
    return pl.pallas_call(
        _proj_kernel,
        grid=(n_steps,),
        in_specs=[row_spec,
                  pl.BlockSpec((1, PROJ_ROWS), lambda i: (0, i)),
                  _resident((ROT_HALF, 1)),
                  _resident((1, D)),
                  _resident((D, N_SECTIONS * D)),
                  _resident((1, D)),
                  _resident((1, 2 * D)),
                  _resident((GM_GROUPS, GM_BLOCK, GM_BLOCK)),
                  _resident((GM_BLOCK, D))] + cast_specs,
        out_specs=[t_spec, row_spec, t_spec, row_spec, row_spec] + cast_specs + [n_spec, n_spec, n_spec],
        out_shape=[jax.ShapeDtypeStruct((B, nq, D, ATTN_BLOCK), bf16),
                   jax.ShapeDtypeStruct((T, D), bf16),
                   jax.ShapeDtypeStruct((B, nq, D, ATTN_BLOCK), bf16),
                   jax.ShapeDtypeStruct((T, D), bf16),
                   jax.ShapeDtypeStruct((T, D), bf16),
                   jax.ShapeDtypeStruct(w_ffn_in.shape, bf16),
                   jax.ShapeDtypeStruct(w_ffn_out.shape, bf16),
                   jax.ShapeDtypeStruct(w_out.shape, bf16),
                   jax.ShapeDtypeStruct((B, nq, n_groups, ATTN_BLOCK), jnp.float32),
                   jax.ShapeDtypeStruct((B, nq, n_groups, ATTN_BLOCK), jnp.float32),
                   jax.ShapeDtypeStruct((B, nq, n_groups, ATTN_BLOCK), jnp.float32)],
        scratch_shapes=[pltpu.VMEM((PROJ_ROWS, D), jnp.float32),
                        pltpu.VMEM((PROJ_ROWS, D), bf16)],
        compiler_params=pltpu.CompilerParams(
            dimension_semantics=("arbitrary",), vmem_limit_bytes=V7X_VMEM_LIMIT_BYTES),
        name="proj",
    )(x2d, pos_row, invf_col, norm_mix_g, w_in.astype(bf16), gm_norm_g,
      gate_b.reshape(1, 2 * D), gm_ws, bs_cols, w_ffn_in, w_ffn_out, w_out)


def _run_attn(qt, k, vt, qn, kn, qk, lambdas, subln_g):
    B, nq, D, _ = qt.shape
    S = nq * ATTN_BLOCK
    assert nq % 2 == 0
    return pl.pallas_call(
        _attn_kernel,
        grid=(B, nq // 2),
        in_specs=[pl.BlockSpec((1, nq, D, ATTN_BLOCK), lambda b, p: (b, 0, 0, 0)),
                  pl.BlockSpec((1, S, D), lambda b, p: (b, 0, 0)),
                  pl.BlockSpec((1, nq, D, ATTN_BLOCK), lambda b, p: (b, 0, 0, 0)),
                  pl.BlockSpec((1, nq, D // HEAD_DIM, ATTN_BLOCK), lambda b, p: (b, 0, 0, 0)),
                  pl.BlockSpec((1, nq, D // HEAD_DIM, ATTN_BLOCK), lambda b, p: (b, 0, 0, 0)),
                  pl.BlockSpec((1, nq, D // HEAD_DIM, ATTN_BLOCK), lambda b, p: (b, 0, 0, 0)),
                  _resident((4, HEAD_DIM)),
                  _resident((1, V_DIM))],
        out_specs=pl.BlockSpec((1, S, D), lambda b, p: (b, 0, 0)),
        out_shape=jax.ShapeDtypeStruct((B, S, D), jnp.bfloat16),
        scratch_shapes=[pltpu.VMEM((2, HEADS, 2, 1, ATTN_BLOCK), jnp.float32),
                        pltpu.VMEM((2, HEADS, 2, ACC_ROWS, ATTN_BLOCK), jnp.float32),
                        pltpu.VMEM((D // HEAD_DIM, ATTN_BLOCK), jnp.float32),
                        pltpu.SMEM((nq // 2,), jnp.int32)],
        compiler_params=pltpu.CompilerParams(
            dimension_semantics=("arbitrary", "arbitrary"),
            vmem_limit_bytes=V7X_VMEM_LIMIT_BYTES),
        name="diff_attn",
    )(qt, k.reshape(B, S, D), vt, qn, kn, qk, lambdas, subln_g)


def _run_tail(x2d, attn2d, gate_a, gm_gated, wo, norm_ffn_g, wfi, wfo, norm_final_g):
    T, D = x2d.shape
    row_spec = pl.BlockSpec((TAIL_ROWS, D), lambda i: (i, 0))
    return pl.pallas_call(
        _tail_kernel,
        grid=(T // TAIL_ROWS,),
        in_specs=[row_spec, row_spec, row_spec, row_spec,
                  _resident((D, D)),
                  _resident((1, D)),
                  _resident((D, 2 * D_FF)),
                  _resident((D_FF, D)),
                  _resident((1, D))],
        out_specs=row_spec,
        out_shape=jax.ShapeDtypeStruct((T, D), jnp.float32),
        compiler_params=pltpu.CompilerParams(
            dimension_semantics=("arbitrary",), vmem_limit_bytes=V7X_VMEM_LIMIT_BYTES),
        name="tail",
    )(x2d, attn2d, gate_a, gm_gated, wo, norm_ffn_g, wfi, wfo, norm_final_g.reshape(1, D))


def kernel(x, positions, norm_mix_g, w_in, gate_b, lambdas, subln_g, gm_norm_g, gm_ws, gm_bs,
           w_out, norm_ffn_g, w_ffn_in, w_ffn_out, norm_final_g):
    B, S, D = x.shape
    T = B * S
    assert D == D_MODEL and S % PROJ_ROWS == 0 and S % ATTN_BLOCK == 0 and S % TAIL_ROWS == 0
    assert norm_mix_g.shape[0] == 1, "single-layer block"
    x2d = x.reshape(T, D)
    qt, k, vt, gate_a, gm_gated, wfi, wfo, wo, qn, kn, qk = _run_proj(
        x2d, positions.reshape(1, T), B, S, norm_mix_g, w_in[0], gm_norm_g, gate_b[0], gm_ws[0], gm_bs[0],
        w_out[0], w_ffn_in[0], w_ffn_out[0])
    attn = _run_attn(qt, k, vt, qn, kn, qk, lambdas[0], subln_g)
    out = _run_tail(x2d, attn.reshape(T, D), gate_a, gm_gated, wo, norm_ffn_g, wfi, wfo, norm_final_g)
    return out.reshape(B, S, D)
```

```python
import math

import jax
import jax.numpy as jnp
import numpy as np
from jax import lax
from jax.experimental import pallas as pl
from jax.experimental.pallas import tpu as pltpu

D_MODEL = 1024
HEADS = 8
HEAD_DIM = 64
V_DIM = 128
ROT_DIM = 16
ROT_HALF = ROT_DIM // 2
ROPE_THETA = 500000.0
CHUNK = 64
GM_GROUPS = 8
GM_BLOCK = 128
D_FF = 2816
EPS = 1e-6
SUBLN_EPS = 1e-5
LAMBDA_INIT = 0.8 - 0.6 * math.exp(-0.3 * 0)
N_SECTIONS = 7

V7X_VMEM_LIMIT_BYTES = 58 * 1024 * 1024

PROJ_ROWS = 512
PROJ_ORDER = ("vn", "u", "gb", "q", "gm", "k", "v", "ga")
ATTN_BLOCK = 256
MAX_FAST_OVERSHOOT = 64.0
BOUND_SLACK_BF16 = 1.0 + 2.0 ** -7
SCORE_LOOKAHEAD = 4
BF16_ROWS_PER_VREG = 16
ACC_ROWS = V_DIM + BF16_ROWS_PER_VREG
WEIGHT_CAST_COLS = 128
TAIL_ROWS = 512
FFN_CHUNKS = ((0, 1536), (1536, 1280))


def _rmsnorm(x, g, eps):
    return (x * lax.rsqrt(jnp.mean(x * x, axis=-1, keepdims=True) + eps)) * g


def _gelu(x):
    return 0.5 * x * (1.0 + lax.erf(x * np.float32(np.sqrt(0.5))))


def _group_sums(x):
    return jnp.concatenate([jnp.sum(x[g * HEAD_DIM:(g + 1) * HEAD_DIM], axis=0, keepdims=True)
                            for g in range(D_MODEL // HEAD_DIM)], axis=0)


def _rope_transposed(xt, cos, sin):
    pieces = []
    for g in range(D_MODEL // HEAD_DIM):
        base = g * HEAD_DIM
        x1 = xt[base:base + ROT_HALF]
        x2 = xt[base + ROT_HALF:base + ROT_DIM]
        pieces.append(x1 * cos - x2 * sin)
        pieces.append(x2 * cos + x1 * sin)
        pieces.append(xt[base + ROT_DIM:base + HEAD_DIM])
    return jnp.concatenate(pieces, axis=0)


def _proj_kernel(x_ref, pos_ref, invf_ref, g_ref, w_ref, gmg_ref, gb_ref, ws_ref, bs_ref,
                 wfi32_ref, wfo32_ref, wo32_ref,
                 qt_ref, k_ref, vt_ref, ga_ref, gmo_ref, wfi_ref, wfo_ref, wo_ref, qn_ref, kn_ref, qk_ref,
                 u_ref, vn_ref):
    x = x_ref[...]
    h = _rmsnorm(x, g_ref[...], EPS).astype(jnp.bfloat16)

    ang = invf_ref[...] * pos_ref[...].astype(jnp.float32)
    cos = jnp.cos(ang)
    sin = jnp.sin(ang)

    n_blk = PROJ_ROWS // ATTN_BLOCK
    n_gm = PROJ_ROWS // GM_BLOCK
    vals = {}

    def section(s):
        return jnp.dot(h, w_ref[:, s * D_MODEL:(s + 1) * D_MODEL],
                       preferred_element_type=jnp.float32)

    def score_stats():
        if "q" in vals and "k" in vals:
            stats = (_group_sums(vals["q"] * vals["q"]), _group_sums(vals["k"] * vals["k"]),
                     _group_sums(vals["q"] * vals["k"]))
            for ref, stat in zip((qn_ref, kn_ref, qk_ref), stats):
                for c in range(n_blk):
                    ref[0, c] = stat[:, c * ATTN_BLOCK:(c + 1) * ATTN_BLOCK]

    def do_q():
        qt = _rope_transposed(section(0).T, cos, sin) * np.float32(HEAD_DIM ** -0.5 * math.log2(math.e))
        qt = qt.astype(jnp.bfloat16)
        vals["q"] = qt.astype(jnp.float32)
        for c in range(n_blk):
            qt_ref[0, c] = qt[:, c * ATTN_BLOCK:(c + 1) * ATTN_BLOCK]
        score_stats()

    def do_k():
        kt = _rope_transposed(section(1).T, cos, sin)
        vals["k"] = kt.astype(jnp.bfloat16).astype(jnp.float32)
        k_ref[...] = kt.T.astype(jnp.bfloat16)
        score_stats()

    def do_v():
        vt = section(2).T.astype(jnp.bfloat16)
        for c in range(n_blk):
            vt_ref[0, c] = vt[:, c * ATTN_BLOCK:(c + 1) * ATTN_BLOCK]

    def do_u():
        u_ref[...] = _gelu(section(3))

    def do_vn():
        vn_ref[...] = _rmsnorm(_gelu(section(4)), gmg_ref[...], EPS).astype(jnp.bfloat16)

    def do_ga():
        ga_ref[...] = jax.nn.sigmoid(section(5) + gb_ref[:, 0:D_MODEL]).astype(jnp.bfloat16)

    def do_gb():
        vals["gate_b"] = jax.nn.sigmoid(section(6) + gb_ref[:, D_MODEL:2 * D_MODEL])

    def do_gm():
        gate_b = vals["gate_b"]
        pi = lax.broadcasted_iota(jnp.int32, (GM_BLOCK, GM_BLOCK), 0) // CHUNK
        pj = lax.broadcasted_iota(jnp.int32, (GM_BLOCK, GM_BLOCK), 1) // CHUNK
        causal = pj <= pi
        for g in range(GM_GROUPS):
            cols = slice(g * GM_BLOCK, (g + 1) * GM_BLOCK)
            wm = jnp.where(causal, ws_ref[g], 0.0).astype(jnp.bfloat16)
            rhs = jnp.concatenate(
                [vn_ref[nb * GM_BLOCK:(nb + 1) * GM_BLOCK, cols] for nb in range(n_gm)], axis=1)
            mix = jnp.dot(wm, rhs, preferred_element_type=jnp.float32)
            for nb in range(n_gm):
                rows = slice(nb * GM_BLOCK, (nb + 1) * GM_BLOCK)
                mixed = mix[:, nb * GM_BLOCK:(nb + 1) * GM_BLOCK] + bs_ref[:, cols]
                gmo_ref[rows, cols] = (gate_b[rows, cols] * (u_ref[rows, cols] * mixed)).astype(jnp.bfloat16)

    jobs = dict(q=do_q, k=do_k, v=do_v, u=do_u, vn=do_vn, ga=do_ga, gb=do_gb, gm=do_gm)
    for name in PROJ_ORDER:
        jobs[name]()

    step = pl.program_id(0)
    for (first, n_blocks), src_ref, dst_ref in zip(_cast_schedule(), (wfi32_ref, wfo32_ref, wo32_ref),
                                                   (wfi_ref, wfo_ref, wo_ref)):
        @pl.when((step >= first) & (step < first + n_blocks))
        def _():
            dst_ref[...] = src_ref[...].astype(jnp.bfloat16)


def _attn_kernel(qt_ref, k_ref, vt_ref, qn_ref, kn_ref, qk_ref, lam_ref, sg_ref, o_ref,
                 m_ref, acc_ref, nb_ref, fast_ref):
    p_id = pl.program_id(1)
    tq = ATTN_BLOCK
    nq = vt_ref.shape[1]
    tile_a = p_id
    tile_b = nq - 1 - p_id
    tiles = (tile_a, tile_b)

    kr = lax.broadcasted_iota(jnp.int32, (tq, tq), 0) // CHUNK
    qc = lax.broadcasted_iota(jnp.int32, (tq, tq), 1) // CHUNK
    allowed = kr <= qc
    ones_rows = jnp.ones((ACC_ROWS - V_DIM, tq), jnp.bfloat16)

    slots = [(1, tile_b, tile_b, True, True)]
    for t in range(1, nq):
        is_b = t <= tile_b
        slots.append((is_b.astype(jnp.int32), jnp.where(is_b, tile_b, tile_a),
                      jnp.where(is_b, tile_b - t, t - (tile_b + 1)), False, False))
    slots.append((0, tile_a, tile_a, True, False))

    items = [(t, h, mp) for t in range(len(slots)) for h in range(HEADS) for mp in range(2)]

    def scores(t, h, mp, fold_offset=False):
        _, tile, blk, _, _ = slots[t]
        rows = pl.ds(pl.multiple_of(blk * tq, tq), tq)
        kb = k_ref[0, rows, h * V_DIM:(h + 1) * V_DIM]
        qh = qt_ref[0, tile, h * V_DIM + mp * HEAD_DIM:h * V_DIM + (mp + 1) * HEAD_DIM, :]
        z = jnp.zeros_like(qh)
        rhs = jnp.concatenate([qh, z] if mp == 0 else [z, qh], axis=0)
        if not fold_offset:
            return jnp.dot(kb, rhs, preferred_element_type=jnp.float32)
        neg = nb_ref[tile, 2 * h + mp]
        pad = jnp.zeros((V_DIM - neg.shape[0], tq), jnp.bfloat16)
        return jnp.dot(jnp.concatenate([kb, one_lane], axis=1), jnp.concatenate([rhs, neg, pad], axis=0),
                       preferred_element_type=jnp.float32)

    def fold_blocks(bounds):
        acc_ref[0] = jnp.zeros(acc_ref.shape[1:], jnp.float32)
        if bounds is None:
            m_ref[0] = jnp.full(m_ref.shape[1:], -jnp.inf, jnp.float32)
        fast = bounds is not None
        pending = [scores(*items[i], fold_offset=fast) for i in range(SCORE_LOOKAHEAD)]
        for idx, (t, h, mp) in enumerate(items):
            sel, _, blk, masked, first = slots[t]
            sm = pending.pop(0)
            if idx + SCORE_LOOKAHEAD < len(items):
                pending.append(scores(*items[idx + SCORE_LOOKAHEAD], fold_offset=fast))
            vta = jnp.concatenate([vt_ref[0, blk, h * V_DIM:(h + 1) * V_DIM, :], ones_rows], axis=0)
            if masked:
                sm = jnp.where(allowed, sm, -jnp.inf)
            if fast:
                offset = alpha = None
            elif first:
                offset = jnp.max(sm, axis=0, keepdims=True)
                alpha = None
            else:
                m_old = m_ref[sel, h, mp]
                offset = jnp.maximum(m_old, jnp.max(sm, axis=0, keepdims=True))
                alpha = jnp.exp2(m_old - offset)
            p = jnp.exp2(sm if fast else sm - offset).astype(jnp.bfloat16)
            pv = jnp.dot(vta, p, preferred_element_type=jnp.float32)
            if bounds is None:
                m_ref[sel, h, mp] = offset
            if first:
                acc_ref[sel, h, mp] = pv
            elif alpha is None:
                acc_ref[sel, h, mp] = acc_ref[sel, h, mp] + pv
            else:
                acc_ref[sel, h, mp] = alpha * acc_ref[sel, h, mp] + pv

    one_lane = (lax.broadcasted_iota(jnp.int32, (tq, V_DIM), 1) == 0).astype(jnp.bfloat16)

    @pl.when(p_id == 0)
    def _():
        kmax = jnp.max(jnp.max(kn_ref[0], axis=0), axis=-1, keepdims=True)
        first_row = lax.broadcasted_iota(jnp.int32, (BF16_ROWS_PER_VREG, tq), 0) == 0
        over = []
        for tile in range(nq):
            b = (jnp.sqrt(qn_ref[0, tile] * kmax) * np.float32(BOUND_SLACK_BF16)).astype(jnp.bfloat16)
            b = b.astype(jnp.float32)
            for g in range(2 * HEADS):
                nb_ref[tile, g] = jnp.where(first_row, -b[g:g + 1], 0.0).astype(jnp.bfloat16)
            over.append(b - qk_ref[0, tile])
        for pp in range(nq // 2):
            o = jnp.maximum(over[pp], over[nq - 1 - pp])
            margin = jnp.max(jnp.max(o, axis=0, keepdims=True), axis=-1, keepdims=True)[0, 0]
            fast_ref[pp] = (margin <= np.float32(MAX_FAST_OVERSHOOT)).astype(jnp.int32)

    bounds = True
    fast_ok = fast_ref[p_id] == 1

    def write_outputs():
        lf = lam_ref[...]
        lam = (jnp.exp(jnp.sum(lf[0:1] * lf[1:2], axis=-1, keepdims=True))
               - jnp.exp(jnp.sum(lf[2:3] * lf[3:4], axis=-1, keepdims=True))
               + np.float32(LAMBDA_INIT))
        for sel in range(2):
            out_rows = pl.ds(pl.multiple_of(tiles[sel] * tq, tq), tq)
            for h in range(HEADS):
                ot = (acc_ref[sel, h, 0, 0:V_DIM] / acc_ref[sel, h, 0, V_DIM:V_DIM + 1]
                      - lam * (acc_ref[sel, h, 1, 0:V_DIM] / acc_ref[sel, h, 1, V_DIM:V_DIM + 1]))
                ot = ot * lax.rsqrt(jnp.mean(ot * ot, axis=0, keepdims=True) + SUBLN_EPS)
                o = (ot.T * sg_ref[...]) * np.float32(1.0 - LAMBDA_INIT)
                o_ref[0, out_rows, h * V_DIM:(h + 1) * V_DIM] = o.astype(o_ref.dtype)

    @pl.when(fast_ok)
    def _():
        fold_blocks(bounds)
        write_outputs()

    @pl.when(jnp.logical_not(fast_ok))
    def _():
        fold_blocks(None)
        write_outputs()


def _tail_kernel(x_ref, attn_ref, ga_ref, gmo_ref, wo_ref, g2_ref, wfi_ref, wfo_ref, gf_ref, out_ref):
    merged = (ga_ref[...].astype(jnp.float32) * attn_ref[...].astype(jnp.float32)
              + gmo_ref[...].astype(jnp.float32))
    x1 = x_ref[...] + jnp.dot(merged.astype(jnp.bfloat16), wo_ref[...],
                              preferred_element_type=jnp.float32)

    h2 = _rmsnorm(x1, g2_ref[...], EPS).astype(jnp.bfloat16)
    ff = None
    for (c0, cw) in FFN_CHUNKS:
        a = jnp.dot(h2, wfi_ref[:, c0:c0 + cw], preferred_element_type=jnp.float32)
        b = jnp.dot(h2, wfi_ref[:, D_FF + c0:D_FF + c0 + cw], preferred_element_type=jnp.float32)
        act = (a * jax.nn.sigmoid(a) * b).astype(jnp.bfloat16)
        part = jnp.dot(act, wfo_ref[c0:c0 + cw, :], preferred_element_type=jnp.float32)
        ff = part if ff is None else ff + part
    x2 = x1 + ff
    out_ref[...] = _rmsnorm(x2, gf_ref[...], EPS)


def _resident(shape):
    return pl.BlockSpec(shape, lambda *_: (0,) * len(shape), pipeline_mode=pl.Buffered(1))


def _cast_schedule():
    counts = (2 * D_FF // WEIGHT_CAST_COLS, D_MODEL // WEIGHT_CAST_COLS, D_MODEL // WEIGHT_CAST_COLS)
    firsts = (0, counts[0], counts[0] + counts[1])
    return tuple(zip(firsts, counts))


def _col_block(rows, first, n_blocks):
    return pl.BlockSpec((rows, WEIGHT_CAST_COLS), lambda i: (0, jnp.clip(i - first, 0, n_blocks - 1)))


def _run_proj(x2d, pos_row, B, S, norm_mix_g, w_in, gm_norm_g, gate_b, gm_ws, gm_bs, w_out, w_ffn_in, w_ffn_out):
    T, D = x2d.shape
    bf16 = jnp.bfloat16
    nq = S // ATTN_BLOCK
    n_steps = T // PROJ_ROWS
    inv_freq = ROPE_THETA ** (-np.arange(0, ROT_DIM, 2, dtype=np.float32) / ROT_DIM)
    invf_col = jnp.asarray(inv_freq, jnp.float32).reshape(ROT_HALF, 1)
    bs_cols = jnp.repeat(gm_bs.T, GM_BLOCK, axis=1)
    sched = _cast_schedule()
    assert sched[-1][0] + sched[-1][1] <= n_steps
    cast_specs = [_col_block(w.shape[0], first, n) for w, (first, n) in zip((w_ffn_in, w_ffn_out, w_out), sched)]

    tiles_per_seq = S // PROJ_ROWS
    blk_per_tile = PROJ_ROWS // ATTN_BLOCK
    row_spec = pl.BlockSpec((PROJ_ROWS, D), lambda i: (i, 0))
    t_spec = pl.BlockSpec((1, blk_per_tile, D, ATTN_BLOCK),
                          lambda i: (i // tiles_per_seq, i % tiles_per_seq, 0, 0))
    n_groups = D // HEAD_DIM
    n_spec = pl.BlockSpec((1, blk_per_tile, n_groups, ATTN_BLOCK),
                          lambda i: (i // tiles_per_seq, i % tiles_per_seq, 0, 0))
    return pl.pallas_call(
        _proj_kernel,
        grid=(n_steps,),
        in_specs=[row_spec,
                  pl.BlockSpec((1, PROJ_ROWS), lambda i: (0, i)),
                  _resident((ROT_HALF, 1)),
                  _resident((1, D)),
                  _resident((D, N_SECTIONS * D)),
                  _resident((1, D)),
                  _resident((1, 2 * D)),
                  _resident((GM_GROUPS, GM_BLOCK, GM_BLOCK)),
                  _resident((GM_BLOCK, D))] + cast_specs,
        out_specs=[t_spec, row_spec, t_spec, row_spec, row_spec] + cast_specs + [n_spec, n_spec, n_spec],
        out_shape=[jax.ShapeDtypeStruct((B, nq, D, ATTN_BLOCK), bf16),
                   jax.ShapeDtypeStruct((T, D), bf16),
                   jax.ShapeDtypeStruct((B, nq, D, ATTN_BLOCK), bf16),
                   jax.ShapeDtypeStruct((T, D), bf16),
                   jax.ShapeDtypeStruct((T, D), bf16),
                   jax.ShapeDtypeStruct(w_ffn_in.shape, bf16),
                   jax.ShapeDtypeStruct(w_ffn_out.shape, bf16),
                   jax.ShapeDtypeStruct(w_out.shape, bf16),
                   jax.ShapeDtypeStruct((B, nq, n_groups, ATTN_BLOCK), jnp.float32),
                   jax.ShapeDtypeStruct((B, nq, n_groups, ATTN_BLOCK), jnp.float32),
                   jax.ShapeDtypeStruct((B, nq, n_groups, ATTN_BLOCK), jnp.float32)],
        scratch_shapes=[pltpu.VMEM((PROJ_ROWS, D), jnp.float32),
                        pltpu.VMEM((PROJ_ROWS, D), bf16)],
        compiler_params=pltpu.CompilerParams(
            dimension_semantics=("arbitrary",), vmem_limit_bytes=V7X_VMEM_LIMIT_BYTES),
        name="proj",
    )(x2d, pos_row, invf_col, norm_mix_g, w_in.astype(bf16), gm_norm_g,
      gate_b.reshape(1, 2 * D), gm_ws, bs_cols, w_ffn_in, w_ffn_out, w_out)


def _run_attn(qt, k, vt, qn, kn, qk, lambdas, subln_g):
    B, nq, D, _ = qt.shape
    S = nq * ATTN_BLOCK
    assert nq % 2 == 0
    return pl.pallas_call(
        _attn_kernel,
        grid=(B, nq // 2),
        in_specs=[pl.BlockSpec((1, nq, D, ATTN_BLOCK), lambda b, p: (b, 0, 0, 0)),
                  pl.BlockSpec((1, S, D), lambda b, p: (b, 0, 0)),
                  pl.BlockSpec((1, nq, D, ATTN_BLOCK), lambda b, p: (b, 0, 0, 0)),
                  pl.BlockSpec((1, nq, D // HEAD_DIM, ATTN_BLOCK), lambda b, p: (b, 0, 0, 0)),
                  pl.BlockSpec((1, nq, D // HEAD_DIM, ATTN_BLOCK), lambda b, p: (b, 0, 0, 0)),
                  pl.BlockSpec((1, nq, D // HEAD_DIM, ATTN_BLOCK), lambda b, p: (b, 0, 0, 0)),
                  _resident((4, HEAD_DIM)),
                  _resident((1, V_DIM))],
        out_specs=pl.BlockSpec((1, S, D), lambda b, p: (b, 0, 0)),
        out_shape=jax.ShapeDtypeStruct((B, S, D), jnp.bfloat16),
        scratch_shapes=[pltpu.VMEM((2, HEADS, 2, 1, ATTN_BLOCK), jnp.float32),
                        pltpu.VMEM((2, HEADS, 2, ACC_ROWS, ATTN_BLOCK), jnp.float32),
                        pltpu.VMEM((nq, D // HEAD_DIM, BF16_ROWS_PER_VREG, ATTN_BLOCK), jnp.bfloat16),
                        pltpu.SMEM((nq // 2,), jnp.int32)],
        compiler_params=pltpu.CompilerParams(
            dimension_semantics=("arbitrary", "arbitrary"),
            vmem_limit_bytes=V7X_VMEM_LIMIT_BYTES),
        name="diff_attn",
    )(qt, k.reshape(B, S, D), vt, qn, kn, qk, lambdas, subln_g)


def _run_tail(x2d, attn2d, gate_a, gm_gated, wo, norm_ffn_g, wfi, wfo, norm_final_g):
    T, D = x2d.shape
    row_spec = pl.BlockSpec((TAIL_ROWS, D), lambda i: (i, 0))
    return pl.pallas_call(
        _tail_kernel,
        grid=(T // TAIL_ROWS,),
        in_specs=[row_spec, row_spec, row_spec, row_spec,
                  _resident((D, D)),
                  _resident((1, D)),
                  _resident((D, 2 * D_FF)),
                  _resident((D_FF, D)),
                  _resident((1, D))],
        out_specs=row_spec,
        out_shape=jax.ShapeDtypeStruct((T, D), jnp.float32),
        compiler_params=pltpu.CompilerParams(
            dimension_semantics=("arbitrary",), vmem_limit_bytes=V7X_VMEM_LIMIT_BYTES),
        name="tail",
    )(x2d, attn2d, gate_a, gm_gated, wo, norm_ffn_g, wfi, wfo, norm_final_g.reshape(1, D))


def kernel(x, positions, norm_mix_g, w_in, gate_b, lambdas, subln_g, gm_norm_g, gm_ws, gm_bs,
           w_out, norm_ffn_g, w_ffn_in, w_ffn_out, norm_final_g):
    B, S, D = x.shape
    T = B * S
    assert D == D_MODEL and S % PROJ_ROWS == 0 and S % ATTN_BLOCK == 0 and S % TAIL_ROWS == 0
    assert norm_mix_g.shape[0] == 1, "single-layer block"
    x2d = x.reshape(T, D)
    qt, k, vt, gate_a, gm_gated, wfi, wfo, wo, qn, kn, qk = _run_proj(
        x2d, positions.reshape(1, T), B, S, norm_mix_g, w_in[0], gm_norm_g, gate_b[0], gm_ws[0], gm_bs[0],
        w_out[0], w_ffn_in[0], w_ffn_out[0])
    attn = _run_attn(qt, k, vt, qn, kn, qk, lambdas[0], subln_g)
    out = _run_tail(x2d, attn.reshape(T, D), gate_a, gm_gated, wo, norm_ffn_g, wfi, wfo, norm_final_g)
    return out.reshape(B, S, D)
```
